```python
import math
import jax, jax.numpy as jnp
from jax import lax
import numpy as np

D_MODEL = 1024
BATCH = 8
SEQ = 4096
DEPTH = 2
DEC_BATCH = 128
DEC_SEQ = 1
PAST_LEN = 16384
PAGE_SIZE = 128

EPS = 1e-6
A_HEADS = 4
A_DH = 64
A_WIDTH = A_HEADS * A_DH
A_CHUNK = 64
B_WINDOWS = (2, 4, 8, 16)
B_GROUPS = 4
B_GDIM = 64
B_WIDTH = B_GROUPS * B_GDIM
B_BUF = 15
C_HEADS = 4
C_NOPE = 64
C_ROPE = 32
C_VDIM = 64
C_Q_RANK = 256
C_KV_RANK = 128
C_WIDTH = C_HEADS * C_VDIM
ROPE_THETA = 10000.0
Q_BLOCK = 128
MLA_SCALE = (C_NOPE + C_ROPE) ** -0.5
D_HEADS = 4
D_HDIM = 64
D_GROUPS = 2
D_STATE = 128
D_CONV = 4
D_WIDTH = D_HEADS * D_HDIM
D_CONV_DIM = D_WIDTH + 2 * D_GROUPS * D_STATE
D_CHUNK = 128
N_BRANCH = 4
BRANCH_WIDTH = 256
FFN_HIDDEN = ((8 * D_MODEL + 3 * 256 - 1) // (3 * 256)) * 256
IN_SPLITS = (A_WIDTH, A_WIDTH, A_WIDTH, A_WIDTH, A_HEADS, A_HEADS, B_WIDTH, C_Q_RANK, C_KV_RANK, C_ROPE, D_WIDTH, D_CONV_DIM, D_HEADS, N_BRANCH * D_MODEL)
IN_WIDTH = sum(IN_SPLITS)

kernel_name = 'hybrid_gated_mlstm_pool_mla_ssd_decoder_step'


def _rmsnorm(x, w):
    xf = x.astype(jnp.float32)
    y = xf * lax.rsqrt(jnp.mean(xf * xf, -1, keepdims=True) + EPS)
    return (y * w.astype(jnp.float32)).astype(x.dtype)


def _rope(x, pos):
    half = x.shape[-1] // 2
    inv = ROPE_THETA ** (-jnp.arange(half, dtype=jnp.float32) / half)
    ang = pos.astype(jnp.float32)[:, None] * inv[None, :]
    shape = (1, ang.shape[0]) + (1,) * (x.ndim - 3) + (half,)
    cos = jnp.cos(ang).reshape(shape)
    sin = jnp.sin(ang).reshape(shape)
    xf = x.astype(jnp.float32)
    x1, x2 = xf[..., :half], xf[..., half:]
    return jnp.concatenate([x1 * cos - x2 * sin, x1 * sin + x2 * cos], -1).astype(x.dtype)


def _mlstm(q, k, v, ig, lf, c0, n0, m0):
    bsz, t_len = q.shape[:2]
    L = A_CHUNK if t_len % A_CHUNK == 0 else t_len
    nc = t_len // L
    causal = jnp.tril(jnp.ones((L, L), bool))

    def chunks(z):
        return z.reshape((bsz, nc, L) + z.shape[2:]).swapaxes(0, 1)

    def step(carry, inp):
        c, n, m = carry
        qc, kc, vc, igc, lfc = inp
        b = jnp.cumsum(lfc, axis=1).transpose(0, 2, 1)
        igt = igc.transpose(0, 2, 1)
        dlog = jnp.where(causal, b[..., :, None] - b[..., None, :] + igt[..., None, :], -jnp.inf)
        inter = b + m[..., None]
        mt = jnp.maximum(jnp.max(dlog, -1), inter)
        s = jnp.exp(dlog - mt[..., None]) * jnp.einsum('blhd,bshd->bhls', qc, kc)
        g = jnp.exp(inter - mt)
        num = jnp.einsum('bhls,bshv->blhv', s, vc) + g.transpose(0, 2, 1)[..., None] * jnp.einsum('blhd,bhdv->blhv', qc, c)
        den = jnp.sum(s, -1) + g * jnp.einsum('blhd,bhd->bhl', qc, n)
        h = num / jnp.maximum(jnp.abs(den), jnp.exp(-mt)).transpose(0, 2, 1)[..., None]
        bl = b[..., -1]
        wlog = bl[..., None] - b + igt
        m_new = jnp.maximum(bl + m, jnp.max(wlog, -1))
        w = jnp.exp(wlog - m_new[..., None])
        decay = jnp.exp(bl + m - m_new)
        c_new = decay[..., None, None] * c + jnp.einsum('bhs,bshd,bshv->bhdv', w, kc, vc)
        n_new = decay[..., None] * n + jnp.einsum('bhs,bshd->bhd', w, kc)
        return (c_new, n_new, m_new), h

    (c1, n1, m1), hs = lax.scan(step, (c0, n0, m0), (chunks(q), chunks(k), chunks(v), chunks(ig), chunks(lf)))
    return hs.swapaxes(0, 1).reshape(q.shape), c1, n1, m1


def _pool(xb, buf, pos, w, scale):
    bsz, t_len, _ = xb.shape
    xp = jnp.concatenate([buf.astype(xb.dtype), xb], 1)
    cs = jnp.pad(jnp.cumsum(xp.astype(jnp.float32), 1), ((0, 0), (1, 0), (0, 0)))
    outs = []
    for g, win in enumerate(B_WINDOWS):
        sl = slice(g * B_GDIM, (g + 1) * B_GDIM)
        tot = cs[:, B_BUF + 1:B_BUF + 1 + t_len, sl] - cs[:, B_BUF + 1 - win:B_BUF + 1 - win + t_len, sl]
        cnt = jnp.minimum(pos + 1, win).astype(jnp.float32)[None, :, None]
        outs.append(tot / cnt - xb[..., sl].astype(jnp.float32))
    pooled = jnp.stack(outs, 2)
    y = jnp.einsum('btgc,gcd->btgd', pooled, w.astype(jnp.float32)).reshape(bsz, t_len, B_WIDTH) * scale.astype(jnp.float32)
    return y.astype(xb.dtype), xp[:, -B_BUF:]


def _mla_prompt(q_abs, q_rope, c, kr):
    bsz, s_len = c.shape[:2]
    kpos = jnp.arange(s_len)

    def block(j):
        qa = lax.dynamic_slice_in_dim(q_abs, j * Q_BLOCK, Q_BLOCK, axis=1)
        qr = lax.dynamic_slice_in_dim(q_rope, j * Q_BLOCK, Q_BLOCK, axis=1)
        s = (jnp.einsum('bthr,bsr->bhts', qa, c) + jnp.einsum('bthe,bse->bhts', qr, kr)).astype(jnp.float32) * MLA_SCALE
        qpos = j * Q_BLOCK + jnp.arange(Q_BLOCK)
        s = jnp.where(kpos[None, :] <= qpos[:, None], s, -jnp.inf)
        p = jax.nn.softmax(s, -1).astype(c.dtype)
        return jnp.einsum('bhts,bsr->bthr', p, c)

    o = lax.map(block, jnp.arange(s_len // Q_BLOCK))
    return o.swapaxes(0, 1).reshape(bsz, s_len, C_HEADS, C_KV_RANK)


def _mla_sample(q_abs, q_rope, c_new, kr_new, c_past, kr_past):
    t_len = q_abs.shape[1]
    past = c_past.shape[1]
    s_past = jnp.einsum('bthr,bsr->bhts', q_abs, c_past) + jnp.einsum('bthe,bse->bhts', q_rope, kr_past)
    s_new = jnp.einsum('bthr,bsr->bhts', q_abs, c_new) + jnp.einsum('bthe,bse->bhts', q_rope, kr_new)
    causal = jnp.tril(jnp.ones((t_len, t_len), bool))
    s_new = jnp.where(causal, s_new.astype(jnp.float32), -jnp.inf)
    s = jnp.concatenate([s_past.astype(jnp.float32), s_new], -1) * MLA_SCALE
    p = jax.nn.softmax(s, -1).astype(c_new.dtype)
    return jnp.einsum('bhts,bsr->bthr', p[..., :past], c_past) + jnp.einsum('bhts,bsr->bthr', p[..., past:], c_new)


def _ssd(x, dt, a, bm, cm, h0):
    bsz, t_len = x.shape[:2]
    rep = D_HEADS // D_GROUPS
    bm = jnp.repeat(bm, rep, axis=2)
    cm = jnp.repeat(cm, rep, axis=2)
    la = dt * a
    xdt = x * dt[..., None]
    L = D_CHUNK if t_len % D_CHUNK == 0 else t_len
    nc = t_len // L
    causal = jnp.tril(jnp.ones((L, L), bool))

    def chunks(z):
        return z.reshape((bsz, nc, L) + z.shape[2:]).swapaxes(0, 1)

    def step(h, inp):
        xc, lac, bc, cc = inp
        cum = jnp.cumsum(lac, axis=1).transpose(0, 2, 1)
        seg = jnp.exp(jnp.where(causal, cum[..., :, None] - cum[..., None, :], -jnp.inf))
        scores = jnp.einsum('blhn,bshn->bhls', cc, bc) * seg
        y = jnp.einsum('bhls,bshp->blhp', scores, xc) + jnp.einsum('blhn,bhpn->blhp', cc, h) * jnp.exp(cum).transpose(0, 2, 1)[..., None]
        tail = jnp.exp(cum[..., -1:] - cum)
        h_new = h * jnp.exp(cum[..., -1])[..., None, None] + jnp.einsum('bshn,bhs,bshp->bhpn', bc, tail, xc)
        return h_new, y

    h1, ys = lax.scan(step, h0, (chunks(xdt), chunks(la), chunks(bm), chunks(cm)))
    return ys.swapaxes(0, 1).reshape(x.shape), h1


def _layer(x, pos0, state, past, p):
    c0, n0, m0, pool_buf, conv_buf, ssm0 = state
    f32 = jnp.float32
    bsz, t_len, _ = x.shape
    dtype = x.dtype
    pos = pos0 + jnp.arange(t_len)
    u = _rmsnorm(x, p['n_mix_pre']) @ p['w_in']
    offs = np.cumsum(IN_SPLITS)[:-1].tolist()
    (aq, ak, av, ao, ai, af, xb, cq, ckv, ckr, dz, dxbc, ddt, gates) = jnp.split(u, offs, axis=-1)

    hd = (bsz, t_len, A_HEADS, A_DH)
    q = aq.reshape(hd).astype(f32)
    k = ak.reshape(hd).astype(f32) * (A_DH ** -0.5)
    v = av.reshape(hd).astype(f32)
    ig = ai.astype(f32) + p['a_bi'].astype(f32)
    lf = jax.nn.log_sigmoid(af.astype(f32) + p['a_bf'].astype(f32))
    h, c1, n1, m1 = _mlstm(q, k, v, ig, lf, c0.astype(f32), n0.astype(f32), m0.astype(f32))
    mu = jnp.mean(h, -1, keepdims=True)
    var = jnp.mean(jnp.square(h - mu), -1, keepdims=True)
    hn = ((h - mu) * lax.rsqrt(var + EPS)).reshape(bsz, t_len, A_WIDTH) * p['a_norm'].astype(f32)
    out_a = (jax.nn.sigmoid(ao.astype(f32)) * hn).astype(dtype)

    out_b, pool1 = _pool(xb, pool_buf, pos, p['b_w'], p['b_scale'])

    ql = _rmsnorm(cq, p['c_qnorm'])
    qf = (ql @ p['c_wuq']).reshape(bsz, t_len, C_HEADS, C_NOPE + C_ROPE)
    q_nope = qf[..., :C_NOPE]
    q_rope = _rope(qf[..., C_NOPE:], pos)
    c_lat = _rmsnorm(ckv, p['c_kvnorm'])
    k_rope = _rope(ckr, pos)
    q_abs = jnp.einsum('bthn,rhn->bthr', q_nope, p['c_wuk'])
    if past is None:
        o_lat = _mla_prompt(q_abs, q_rope, c_lat, k_rope)
    else:
        o_lat = _mla_sample(q_abs, q_rope, c_lat, k_rope, past[0], past[1])
    out_c = jnp.einsum('bthr,rhv->bthv', o_lat, p['c_wuv']).reshape(bsz, t_len, C_WIDTH)

    xpad = jnp.concatenate([conv_buf.astype(dtype), dxbc], 1)
    conv = p['d_conv_b'] + xpad[:, 0:t_len] * p['d_conv_w'][0]
    for j in range(1, D_CONV):
        conv = conv + xpad[:, j:j + t_len] * p['d_conv_w'][j]
    conv = jax.nn.silu(conv)
    conv1 = xpad[:, -(D_CONV - 1):]
    gn = D_GROUPS * D_STATE
    x4 = conv[..., :D_WIDTH].reshape(bsz, t_len, D_HEADS, D_HDIM).astype(f32)
    bm = conv[..., D_WIDTH:D_WIDTH + gn].reshape(bsz, t_len, D_GROUPS, D_STATE).astype(f32)
    cm = conv[..., D_WIDTH + gn:].reshape(bsz, t_len, D_GROUPS, D_STATE).astype(f32)
    dtv = jax.nn.softplus(ddt.astype(f32) + p['d_dt_bias'].astype(f32))
    a = -jnp.exp(p['d_a_log'].astype(f32))
    y, ssm1 = _ssd(x4, dtv, a, bm, cm, ssm0.astype(f32))
    y = y + p['d_skip'].astype(f32)[:, None] * x4
    y = y.reshape(bsz, t_len, D_WIDTH) * jax.nn.silu(dz.astype(f32))
    yg = y.reshape(bsz, t_len, D_GROUPS, D_WIDTH // D_GROUPS)
    yg = yg * lax.rsqrt(jnp.mean(yg * yg, -1, keepdims=True) + EPS)
    out_d = (yg.reshape(bsz, t_len, D_WIDTH) * p['d_norm'].astype(f32)).astype(dtype)

    gates = gates.reshape(bsz, t_len, N_BRANCH, D_MODEL)
    terms = [jax.nn.sigmoid(gates[:, :, i]) * (o_i @ p['w_branch'][i]) for i, o_i in enumerate((out_a, out_b, out_c, out_d))]
    mixed = terms[0] + terms[1] + terms[2] + terms[3]
    x = x + _rmsnorm(mixed @ p['w_o'], p['n_mix_post'])

    gu = _rmsnorm(x, p['n_ffn_pre']) @ p['w_gu']
    f = (jax.nn.silu(gu[..., :FFN_HIDDEN]) * gu[..., FFN_HIDDEN:]) @ p['w_down']
    x = x + _rmsnorm(f, p['n_ffn_post'])
    new_state = (c1.astype(dtype), n1.astype(dtype), m1.astype(dtype), pool1, conv1, ssm1.astype(dtype))
    return x, new_state, (c_lat, k_rope)


def setup_inputs(seed: int = 0) -> dict:
    f32 = jnp.float32
    key = jax.random.key(seed)
    keys = list(jax.random.split(key, 48))

    def nk():
        return keys.pop()

    def nrm(shape, scale=1.0):
        return jax.random.normal(nk(), shape, f32) * scale

    def gain(shape):
        return 1.0 + nrm(shape, 0.02)

    n_pages = PAST_LEN // PAGE_SIZE
    n_pool = (DEC_BATCH * n_pages * 5) // 4
    page_table = jax.random.permutation(nk(), n_pool)[:DEC_BATCH * n_pages].reshape(DEC_BATCH, n_pages).astype(jnp.int32)
    dt0 = jnp.exp(jax.random.uniform(nk(), (DEPTH, D_HEADS), f32, math.log(1e-3), math.log(1e-1)))
    return {
        'x_prompt': nrm((BATCH, SEQ, D_MODEL)),
        'x_sample': nrm((DEC_BATCH, DEC_SEQ, D_MODEL)),
        'state_mlstm_c': nrm((DEPTH, DEC_BATCH, A_HEADS, A_DH, A_DH)),
        'state_mlstm_n': nrm((DEPTH, DEC_BATCH, A_HEADS, A_DH)),
        'state_mlstm_m': nrm((DEPTH, DEC_BATCH, A_HEADS)),
        'state_pool': nrm((DEPTH, DEC_BATCH, B_BUF, B_WIDTH)),
        'cache_ckv': nrm((DEPTH, n_pool, PAGE_SIZE, C_KV_RANK)),
        'cache_krope': nrm((DEPTH, n_pool, PAGE_SIZE, C_ROPE)),
        'page_table': page_table,
        'state_conv': nrm((DEPTH, DEC_BATCH, D_CONV - 1, D_CONV_DIM)),
        'state_ssm': nrm((DEPTH, DEC_BATCH, D_HEADS, D_HDIM, D_STATE), 0.5),
        'n_mix_pre': gain((DEPTH, D_MODEL)),
        'n_mix_post': gain((DEPTH, D_MODEL)),
        'n_ffn_pre': gain((DEPTH, D_MODEL)),
        'n_ffn_post': gain((DEPTH, D_MODEL)),
        'w_in': nrm((DEPTH, D_MODEL, IN_WIDTH), D_MODEL ** -0.5),
        'a_bi': nrm((DEPTH, A_HEADS), 0.1),
        'a_bf': jnp.broadcast_to(jnp.linspace(3.0, 6.0, A_HEADS, dtype=f32), (DEPTH, A_HEADS)) + nrm((DEPTH, A_HEADS), 0.01),
        'a_norm': gain((DEPTH, A_WIDTH)),
        'b_w': nrm((DEPTH, B_GROUPS, B_GDIM, B_GDIM), B_GDIM ** -0.5),
        'b_scale': gain((DEPTH, B_WIDTH)),
        'c_qnorm': gain((DEPTH, C_Q_RANK)),
        'c_wuq': nrm((DEPTH, C_Q_RANK, C_HEADS * (C_NOPE + C_ROPE)), C_Q_RANK ** -0.5),
        'c_kvnorm': gain((DEPTH, C_KV_RANK)),
        'c_wuk': nrm((DEPTH, C_KV_RANK, C_HEADS, C_NOPE), C_KV_RANK ** -0.5),
        'c_wuv': nrm((DEPTH, C_KV_RANK, C_HEADS, C_VDIM), C_KV_RANK ** -0.5),
        'd_conv_w': nrm((DEPTH, D_CONV, D_CONV_DIM), D_CONV ** -0.5),
        'd_conv_b': nrm((DEPTH, D_CONV_DIM), 0.02),
        'd_dt_bias': dt0 + jnp.log(-jnp.expm1(-dt0)),
        'd_a_log': jnp.log(jax.random.uniform(nk(), (DEPTH, D_HEADS), f32, 1.0, 16.0)),
        'd_skip': 1.0 + nrm((DEPTH, D_HEADS), 0.1),
        'd_norm': gain((DEPTH, D_WIDTH)),
        'w_branch': nrm((DEPTH, N_BRANCH, BRANCH_WIDTH, D_MODEL), BRANCH_WIDTH ** -0.5),
        'w_o': nrm((DEPTH, D_MODEL, D_MODEL), D_MODEL ** -0.5),
        'w_gu': nrm((DEPTH, D_MODEL, 2 * FFN_HIDDEN), D_MODEL ** -0.5),
        'w_down': nrm((DEPTH, FFN_HIDDEN, D_MODEL), FFN_HIDDEN ** -0.5),
    }


def reference(x_prompt, x_sample, state_mlstm_c, state_mlstm_n, state_mlstm_m, state_pool, cache_ckv, cache_krope, page_table, state_conv, state_ssm,
              n_mix_pre, n_mix_post, n_ffn_pre, n_ffn_post, w_in, a_bi, a_bf, a_norm, b_w, b_scale, c_qnorm, c_wuq, c_kvnorm, c_wuk, c_wuv,
              d_conv_w, d_conv_b, d_dt_bias, d_a_log, d_skip, d_norm, w_branch, w_o, w_gu, w_down):
    f32 = jnp.float32
    bp = x_prompt.shape[0]
    bs = x_sample.shape[0]
    past_len = page_table.shape[1] * cache_ckv.shape[2]
    yp, ys = x_prompt, x_sample
    ckv_p, kr_p, ckv_s, kr_s = [], [], [], []
    st_p_all, st_s_all = [], []
    for l in range(DEPTH):
        p = {'n_mix_pre': n_mix_pre[l], 'n_mix_post': n_mix_post[l], 'n_ffn_pre': n_ffn_pre[l], 'n_ffn_post': n_ffn_post[l],
             'w_in': w_in[l], 'a_bi': a_bi[l], 'a_bf': a_bf[l], 'a_norm': a_norm[l], 'b_w': b_w[l], 'b_scale': b_scale[l],
             'c_qnorm': c_qnorm[l], 'c_wuq': c_wuq[l], 'c_kvnorm': c_kvnorm[l], 'c_wuk': c_wuk[l], 'c_wuv': c_wuv[l],
             'd_conv_w': d_conv_w[l], 'd_conv_b': d_conv_b[l], 'd_dt_bias': d_dt_bias[l], 'd_a_log': d_a_log[l], 'd_skip': d_skip[l],
             'd_norm': d_norm[l], 'w_branch': w_branch[l], 'w_o': w_o[l], 'w_gu': w_gu[l], 'w_down': w_down[l]}
        zero_state = (jnp.zeros((bp, A_HEADS, A_DH, A_DH), f32), jnp.zeros((bp, A_HEADS, A_DH), f32), jnp.zeros((bp, A_HEADS), f32),
                      jnp.zeros((bp, B_BUF, B_WIDTH), x_prompt.dtype), jnp.zeros((bp, D_CONV - 1, D_CONV_DIM), x_prompt.dtype),
                      jnp.zeros((bp, D_HEADS, D_HDIM, D_STATE), f32))
        yp, st_p, rows_p = _layer(yp, 0, zero_state, None, p)
        c_past = cache_ckv[l][page_table].reshape(bs, past_len, C_KV_RANK)
        kr_past = cache_krope[l][page_table].reshape(bs, past_len, C_ROPE)
        in_state = (state_mlstm_c[l], state_mlstm_n[l], state_mlstm_m[l], state_pool[l], state_conv[l], state_ssm[l])
        ys, st_s, rows_s = _layer(ys, past_len, in_state, (c_past, kr_past), p)
        ckv_p.append(rows_p[0]); kr_p.append(rows_p[1])
        ckv_s.append(rows_s[0]); kr_s.append(rows_s[1])
        st_p_all.append(st_p); st_s_all.append(st_s)

    def stk(lst, i):
        return jnp.stack([s[i] for s in lst], 0)

    return (yp, ys,
            jnp.stack(ckv_p, 0), jnp.stack(kr_p, 0), jnp.stack(ckv_s, 0), jnp.stack(kr_s, 0),
            stk(st_p_all, 0), stk(st_p_all, 1), stk(st_p_all, 2), stk(st_s_all, 0), stk(st_s_all, 1), stk(st_s_all, 2),
            stk(st_p_all, 3), stk(st_s_all, 3),
            stk(st_p_all, 4), stk(st_s_all, 4),
            stk(st_p_all, 5), stk(st_s_all, 5))
```

```python
import functools
import math

import jax
import jax.numpy as jnp
import numpy as np
from jax import lax
from jax.experimental import pallas as pl
from jax.experimental.pallas import tpu as pltpu

F32 = jnp.float32
BF16 = jnp.bfloat16
HIGHEST = lax.Precision.HIGHEST
NEG_INF = float("-inf")

EPS = 1e-6
D_MODEL = 1024
HEADS = 4
DH = 64
WIDTH = 256
C_ROPE = 32
C_KV_RANK = 128
ROPE_THETA = 10000.0
MLA_SCALE = (64 + C_ROPE) ** -0.5
D_STATE = 128
D_CONV = 4
D_CONV_DIM = 768
POOL_WINDOWS = (2, 4, 8, 16)
POOL_BUF = 15
FFN_HIDDEN = 2816
IN_SPLITS = (256, 256, 256, 256, 4, 4, 256, 256, 128, 32, 256, 768, 4, 4096)

U_MLSTM = 0
U_XB = 1024
U_CQ = 1280
U_DXBC = 1536
U_DZ = 2304
U_CKV = 2560
U_SMALL = 2688
U_K1 = 2816
U_K2 = 2944
U_GATES = 3072
U_WIDTH = 7168
SM_IG, SM_FG, SM_DT = 32, 40, 48

VMEM_LIMIT = 56 * 1024 * 1024
CHUNK = 256
PAGES_PER_STEP = 16


def _cparams(*sem):
    return pltpu.CompilerParams(dimension_semantics=sem, vmem_limit_bytes=VMEM_LIMIT)


def _rms(xf, g):
    return xf * lax.rsqrt(jnp.mean(xf * xf, -1, keepdims=True) + EPS) * g


def _softplus(x):
    return jnp.maximum(x, 0.0) + jnp.log1p(jnp.exp(-jnp.abs(x)))


def _log_sigmoid(x):
    return -_softplus(-x)


def _silu(x):
    return x * jax.nn.sigmoid(x)


def _dot(a, b):
    return jnp.dot(a, b, preferred_element_type=F32)


def _dot_nt(a, b):
    return lax.dot_general(a, b, (((1,), (1,)), ((), ())), preferred_element_type=F32)


def _dot_exact(a, b):
    return jnp.dot(a, b, precision=HIGHEST, preferred_element_type=F32)


def _tri(n):
    row = lax.broadcasted_iota(jnp.int32, (n, n), 0)
    col = lax.broadcasted_iota(jnp.int32, (n, n), 1)
    return row >= col, row <= col


def _in_proj_kernel(x_ref, g_ref, w_ref, o_ref, xn_ref):
    @pl.when(pl.program_id(1) == 0)
    def _():
        xn_ref[...] = _rms(x_ref[...], g_ref[...]).astype(BF16)

    o_ref[...] = _dot(xn_ref[...], w_ref[...])


def _in_proj(x, g, w):
    n, d = x.shape
    wn = w.shape[1]
    tm = min(n, 1024)
    tn = 1792
    return pl.pallas_call(
        _in_proj_kernel,
        grid=(n // tm, wn // tn),
        in_specs=[pl.BlockSpec((tm, d), lambda i, j: (i, 0)),
                  pl.BlockSpec((1, d), lambda i, j: (0, 0)),
                  pl.BlockSpec((d, tn), lambda i, j: (0, j))],
        out_specs=pl.BlockSpec((tm, tn), lambda i, j: (i, j)),
        out_shape=jax.ShapeDtypeStruct((n, wn), F32),
        scratch_shapes=[pltpu.VMEM((tm, d), BF16)],
        compiler_params=_cparams("parallel", "arbitrary"),
        name="in_proj",
    )(x, g, w)


def _mlstm_kernel(qkvo_ref, sm_ref, bias_ref, norm_ref, out_ref, cst_ref, mst_ref, c_scr, m_scr):
    j = pl.program_id(1)
    L = qkvo_ref.shape[0]

    @pl.when(j == 0)
    def _():
        c_scr[...] = jnp.zeros_like(c_scr)
        m_scr[...] = jnp.zeros_like(m_scr)

    pre = sm_ref[...] + bias_ref[...]
    causal, anti = _tri(L)
    tril = causal.astype(F32)
    triu = anti.astype(F32)
    bcum = _dot_exact(tril, _log_sigmoid(pre))
    pre_t = pre.T
    ig_t = pre_t[SM_IG:SM_IG + 8, :]
    b_t = _dot_exact(_log_sigmoid(pre_t[SM_FG:SM_FG + 8, :]), triu)
    k_t = qkvo_ref[:, 256:512].T
    lane = lax.broadcasted_iota(jnp.int32, (L, 128), 1)
    sub = lax.broadcasted_iota(jnp.int32, (128, L), 0)

    for pair in range(2):
        hn_pair = jnp.zeros((L, 128), F32)
        for half in range(2):
            h = 2 * pair + half
            lo = 64 * half
            qq = qkvo_ref[:, 128 * pair:128 * pair + 128]
            kk = qkvo_ref[:, 256 + 128 * pair:256 + 128 * pair + 128]
            vv = qkvo_ref[:, 512 + 128 * pair:512 + 128 * pair + 128]
            inhalf = (lane >= lo) & (lane < lo + 64)
            qm = jnp.where(inhalf, qq, 0.0).astype(BF16)
            sc = _dot_nt(qm, kk.astype(BF16)) * 0.125
            bcol = bcum[:, SM_FG + h:SM_FG + h + 1]
            brow = b_t[h:h + 1, :]
            igrow = ig_t[h:h + 1, :]
            dlog = jnp.where(causal, bcol - brow + igrow, NEG_INF)
            m_h = m_scr[h:h + 1, 0:1]
            inter = bcol + m_h
            mt = jnp.maximum(jnp.max(dlog, -1, keepdims=True), inter)
            s = jnp.exp(dlog - mt) * sc
            g = jnp.exp(inter - mt)
            vsh = vv if half == 0 else pltpu.roll(vv, 64, 1)
            vaug = jnp.where(lane < 64, vsh, jnp.where(lane == 64, 1.0, 0.0)).astype(BF16)
            cp = c_scr[h]
            nd = _dot(s.astype(BF16), vaug) + g * _dot(qm, cp.astype(BF16))
            den = nd[:, 64:65]
            hh = nd / jnp.maximum(jnp.abs(den), jnp.exp(-mt))
            valid = lane < 64
            mu = jnp.sum(jnp.where(valid, hh, 0.0), -1, keepdims=True) * (1.0 / 64)
            dv = jnp.where(valid, hh - mu, 0.0)
            var = jnp.sum(dv * dv, -1, keepdims=True) * (1.0 / 64)
            hn = dv * lax.rsqrt(var + EPS)
            hn_pair = hn_pair + (hn if half == 0 else pltpu.roll(hn, 64, 1))
            bl = bcol[L - 1:L, :]
            wlog = bl - brow + igrow
            m_new = jnp.maximum(bl + m_h, jnp.max(wlog, -1, keepdims=True))
            w = jnp.exp(wlog - m_new)
            decay = jnp.exp(bl + m_h - m_new)
            inrows = (sub >= lo) & (sub < lo + 64)
            ktw = jnp.where(inrows, k_t[128 * pair:128 * pair + 128, :], 0.0) * (w * 0.125)
            c_scr[h] = decay * cp + _dot(ktw.astype(BF16), vaug)
            m_scr[h:h + 1, :] = jnp.broadcast_to(m_new, (1, 128))
        og = qkvo_ref[:, 768 + 128 * pair:768 + 128 * pair + 128]
        out_ref[:, 128 * pair:128 * pair + 128] = (
            jax.nn.sigmoid(og) * (hn_pair * norm_ref[:, 128 * pair:128 * pair + 128]))

    @pl.when(j == pl.num_programs(1) - 1)
    def _():
        cst_ref[0] = c_scr[...]
        mst_ref[0] = m_scr[...]


def _mlstm_prompt(u, bsz, t_len, bias_row, norm_row):
    L = min(CHUNK, t_len)
    nc = t_len // L
    n = bsz * t_len
    return pl.pallas_call(
        _mlstm_kernel,
        grid=(bsz, nc),
        in_specs=[pl.BlockSpec((L, 1024), lambda b, j: (b * nc + j, U_MLSTM // 1024)),
                  pl.BlockSpec((L, 128), lambda b, j: (b * nc + j, U_SMALL // 128)),
                  pl.BlockSpec((1, 128), lambda b, j: (0, 0)),
                  pl.BlockSpec((1, WIDTH), lambda b, j: (0, 0))],
        out_specs=[pl.BlockSpec((L, WIDTH), lambda b, j: (b * nc + j, 0)),
                   pl.BlockSpec((1, HEADS, 128, 128), lambda b, j: (b, 0, 0, 0)),
                   pl.BlockSpec((1, 8, 128), lambda b, j: (b, 0, 0))],
        out_shape=[jax.ShapeDtypeStruct((n, WIDTH), F32),
                   jax.ShapeDtypeStruct((bsz, HEADS, 128, 128), F32),
                   jax.ShapeDtypeStruct((bsz, 8, 128), F32)],
        scratch_shapes=[pltpu.VMEM((HEADS, 128, 128), F32), pltpu.VMEM((8, 128), F32)],
        compiler_params=_cparams("parallel", "arbitrary"),
        name="mlstm_prompt",
    )(u, u, bias_row, norm_row)


def _pool_kernel(xb_ref, w_ref, scale_ref, out_ref, st_ref, xs):
    j = pl.program_id(1)
    tt = xb_ref.shape[0]

    @pl.when(j == 0)
    def _():
        xs[0:16, :] = jnp.zeros((16, WIDTH), F32)

    xb = xb_ref[...]
    xs[16:16 + tt, :] = xb
    lane = lax.broadcasted_iota(jnp.int32, (tt, WIDTH), 1)
    pos = j * tt + lax.broadcasted_iota(jnp.int32, (tt, WIDTH), 0)
    acc = xb
    tot = jnp.zeros((tt, WIDTH), F32)
    cnt = jnp.zeros((tt, WIDTH), F32)
    for k in range(1, 16):
        acc = acc + xs[pl.ds(16 - k, tt), :]
        if k + 1 in POOL_WINDOWS:
            g = POOL_WINDOWS.index(k + 1)
            sel = (lane >= 64 * g) & (lane < 64 * g + 64)
            tot = jnp.where(sel, acc, tot)
            cnt = jnp.where(sel, jnp.minimum(pos + 1, k + 1).astype(F32), cnt)
    pooled = tot / cnt - xb
    out_ref[...] = _dot(pooled.astype(BF16), w_ref[...]) * scale_ref[...]
    xs[0:16, :] = xs[tt:tt + 16, :]

    @pl.when(j == pl.num_programs(1) - 1)
    def _():
        st_ref[0] = xs[1:16, :]


def _pool_prompt(u, bsz, t_len, w_blk, scale_row):
    tt = min(512, t_len)
    nt = t_len // tt
    n = bsz * t_len
    return pl.pallas_call(
        _pool_kernel,
        grid=(bsz, nt),
        in_specs=[pl.BlockSpec((tt, WIDTH), lambda b, j: (b * nt + j, U_XB // WIDTH)),
                  pl.BlockSpec((WIDTH, WIDTH), lambda b, j: (0, 0)),
                  pl.BlockSpec((1, WIDTH), lambda b, j: (0, 0))],
        out_specs=[pl.BlockSpec((tt, WIDTH), lambda b, j: (b * nt + j, 0)),
                   pl.BlockSpec((1, POOL_BUF, WIDTH), lambda b, j: (b, 0, 0))],
        out_shape=[jax.ShapeDtypeStruct((n, WIDTH), F32),
                   jax.ShapeDtypeStruct((bsz, POOL_BUF, WIDTH), F32)],
        scratch_shapes=[pltpu.VMEM((tt + 16, WIDTH), F32)],
        compiler_params=_cparams("parallel", "arbitrary"),
        name="pool_prompt",
    )(u, w_blk, scale_row)


def _ssd_kernel(dxbc_ref, dz_ref, sm_ref, cw_ref, cb_ref, bias_ref, alog_row_ref, alog_col_ref,
                skip_ref, norm_ref, out_ref, cst_ref, hst_ref, xs, hs):
    j = pl.program_id(1)
    L = dxbc_ref.shape[0]

    @pl.when(j == 0)
    def _():
        xs[0:8, :] = jnp.zeros((8, D_CONV_DIM), F32)
        hs[...] = jnp.zeros_like(hs)

    xs[8:8 + L, :] = dxbc_ref[...]
    conv = cb_ref[...] + xs[pl.ds(5, L), :] * cw_ref[0:1, :]
    for tap in range(1, D_CONV):
        conv = conv + xs[pl.ds(5 + tap, L), :] * cw_ref[tap:tap + 1, :]
    conv = _silu(conv)
    xs[0:8, :] = xs[L:L + 8, :]
    x4 = conv[:, 0:256]

    pre = sm_ref[...] + bias_ref[...]
    dtf = _softplus(pre)
    causal, anti = _tri(L)
    cum = _dot_exact(causal.astype(F32), dtf * (-jnp.exp(alog_row_ref[...])))
    dt_t = _softplus(pre.T[SM_DT:SM_DT + 8, :])
    cum_t = _dot_exact(dt_t * (-jnp.exp(alog_col_ref[...])), anti.astype(F32))

    lane = lax.broadcasted_iota(jnp.int32, (L, WIDTH), 1)
    dtb = jnp.zeros((L, WIDTH), F32)
    cumb = jnp.zeros((L, WIDTH), F32)
    for h in range(HEADS):
        sel = (lane >= 64 * h) & (lane < 64 * h + 64)
        dtb = jnp.where(sel, dtf[:, SM_DT + h:SM_DT + h + 1], dtb)
        cumb = jnp.where(sel, cum[:, SM_DT + h:SM_DT + h + 1], cumb)
    xdt = (x4 * dtb).astype(BF16)

    y = jnp.zeros((L, WIDTH), F32)
    y_inter = []
    for g in range(2):
        cg = conv[:, 512 + 128 * g:512 + 128 * g + 128].astype(BF16)
        bg = conv[:, 256 + 128 * g:256 + 128 * g + 128].astype(BF16)
        cb = _dot_nt(cg, bg)
        for hl in range(2):
            h = 2 * g + hl
            ccol = cum[:, SM_DT + h:SM_DT + h + 1]
            crow = cum_t[h:h + 1, :]
            seg = jnp.exp(jnp.where(causal, ccol - crow, NEG_INF))
            yh = _dot((cb * seg).astype(BF16), xdt)
            y = jnp.where((lane >= 64 * h) & (lane < 64 * h + 64), yh, y)
        y_inter.append(_dot_nt(cg, hs[128 * g:128 * g + 128, :].astype(BF16)))
    y = y + jnp.concatenate(y_inter, axis=1) * jnp.exp(cumb) + skip_ref[...] * x4
    y = y * _silu(dz_ref[...])
    for g in range(2):
        yg = y[:, 128 * g:128 * g + 128]
        yg = yg * lax.rsqrt(jnp.mean(yg * yg, -1, keepdims=True) + EPS)
        out_ref[:, 128 * g:128 * g + 128] = yg * norm_ref[:, 128 * g:128 * g + 128]

    x4_t = x4.T
    sub = lax.broadcasted_iota(jnp.int32, (128, L), 0)
    subc = lax.broadcasted_iota(jnp.int32, (128, 1), 0)
    for g in range(2):
        bg = conv[:, 256 + 128 * g:256 + 128 * g + 128].astype(BF16)
        rows, decs = [], []
        for hl in range(2):
            h = 2 * g + hl
            last = cum_t[h:h + 1, L - 1:L]
            rows.append(dt_t[h:h + 1, :] * jnp.exp(last - cum_t[h:h + 1, :]))
            decs.append(jnp.exp(last))
        x2t = x4_t[128 * g:128 * g + 128, :] * jnp.where(sub < 64, rows[0], rows[1])
        dec = jnp.where(subc < 64, decs[0], decs[1])
        hs[128 * g:128 * g + 128, :] = hs[128 * g:128 * g + 128, :] * dec + _dot(x2t.astype(BF16), bg)

    @pl.when(j == pl.num_programs(1) - 1)
    def _():
        cst_ref[0] = xs[5:8, :]
        hst_ref[0] = hs[...]


def _ssd_prompt(u, bsz, t_len, cw, cb, bias_row, alog_row, alog_col, skip_row, norm_row):
    L = min(CHUNK, t_len)
    nc = t_len // L
    n = bsz * t_len
    const = lambda shape: pl.BlockSpec(shape, lambda b, j: (0,) * len(shape))
    return pl.pallas_call(
        _ssd_kernel,
        grid=(bsz, nc),
        in_specs=[pl.BlockSpec((L, D_CONV_DIM), lambda b, j: (b * nc + j, U_DXBC // D_CONV_DIM)),
                  pl.BlockSpec((L, WIDTH), lambda b, j: (b * nc + j, U_DZ // WIDTH)),
                  pl.BlockSpec((L, 128), lambda b, j: (b * nc + j, U_SMALL // 128)),
                  const((D_CONV, D_CONV_DIM)), const((1, D_CONV_DIM)), const((1, 128)), const((1, 128)),
                  const((8, 1)), const((1, WIDTH)), const((1, WIDTH))],
        out_specs=[pl.BlockSpec((L, WIDTH), lambda b, j: (b * nc + j, 0)),
                   pl.BlockSpec((1, D_CONV - 1, D_CONV_DIM), lambda b, j: (b, 0, 0)),
                   pl.BlockSpec((1, HEADS * DH, D_STATE), lambda b, j: (b, 0, 0))],
        out_shape=[jax.ShapeDtypeStruct((n, WIDTH), F32),
                   jax.ShapeDtypeStruct((bsz, D_CONV - 1, D_CONV_DIM), F32),
                   jax.ShapeDtypeStruct((bsz, HEADS * DH, D_STATE), F32)],
        scratch_shapes=[pltpu.VMEM((L + 8, D_CONV_DIM), F32), pltpu.VMEM((HEADS * DH, D_STATE), F32)],
        compiler_params=_cparams("parallel", "arbitrary"),
        name="ssd_prompt",
    )(u, u, u, cw, cb, bias_row, alog_row, alog_col, skip_row, norm_row)


def _mla_prep_kernel(cq_ref, ckv_ref, k1_ref, k2_ref, cs1_ref, cs2_ref, csa_ref, csb_ref, qn_ref, kvn_ref,
                     wn_ref, wx1_ref, wx2_ref, wuk_ref, hm_ref, clat_ref, kr_ref, kcat_ref, q_ref):
    ql = _rms(cq_ref[...], qn_ref[...]).astype(BF16)
    qnope = _dot(ql, wn_ref[...])
    cs1 = cs1_ref[...]
    cs2 = cs2_ref[...]
    qt = _dot(ql, wx1_ref[...]) * cs1 + _dot(ql, wx2_ref[...]) * cs2
    qabs = _dot(qnope.astype(BF16), wuk_ref[...])
    for h in range(HEADS):
        q_ref[h, :, 0:128] = qabs[:, 128 * h:128 * h + 128].astype(BF16)
        q_ref[h, :, 128:256] = (qt * hm_ref[h:h + 1, :]).astype(BF16)
    clat = _rms(ckv_ref[...], kvn_ref[...])
    clat_ref[...] = clat
    k1 = k1_ref[...]
    k2 = k2_ref[...]
    kcat_ref[:, 0:128] = clat.astype(BF16)
    kcat_ref[:, 128:256] = (k1 * cs1 + k2 * cs2).astype(BF16)
    kr_ref[...] = k1[:, 0:C_ROPE] * csa_ref[...] + k2[:, 0:C_ROPE] * csb_ref[...]


def _mla_prep(u, t_len, tabs, qn, kvn, wn, wx1, wx2, wuk, hm):
    n = u.shape[0]
    tm = min(512, t_len, n)
    rows = tabs[0].shape[0]
    if rows == 1:
        tab = lambda w: pl.BlockSpec((1, w), lambda i: (0, 0))
    else:
        ntab = rows // tm
        tab = lambda w: pl.BlockSpec((tm, w), lambda i: (i % ntab, 0))
    const = lambda shape: pl.BlockSpec(shape, lambda i: (0,) * len(shape))
    col = lambda w, off: pl.BlockSpec((tm, w), lambda i: (i, off // w))
    return pl.pallas_call(
        _mla_prep_kernel,
        grid=(n // tm,),
        in_specs=[col(256, U_CQ), col(128, U_CKV), col(128, U_K1), col(128, U_K2),
                  tab(128), tab(128), tab(C_ROPE), tab(C_ROPE),
                  const((1, 256)), const((1, 128)), const((256, 256)), const((256, 128)), const((256, 128)),
                  const((256, 512)), const((HEADS, 128))],
        out_specs=[pl.BlockSpec((tm, 128), lambda i: (i, 0)),
                   pl.BlockSpec((tm, C_ROPE), lambda i: (i, 0)),
                   pl.BlockSpec((tm, 256), lambda i: (i, 0)),
                   pl.BlockSpec((HEADS, tm, 256), lambda i: (0, i, 0))],
        out_shape=[jax.ShapeDtypeStruct((n, 128), F32),
                   jax.ShapeDtypeStruct((n, C_ROPE), F32),
                   jax.ShapeDtypeStruct((n, 256), BF16),
                   jax.ShapeDtypeStruct((HEADS, n, 256), BF16)],
        compiler_params=_cparams("parallel"),
        name="mla_prep",
    )(u, u, u, u, *tabs, qn, kvn, wn, wx1, wx2, wuk, hm)


def _flash_kernel(q_ref, k_ref, wuv_ref, o_ref, m_scr, l_scr, acc_scr):
    qi = pl.program_id(1)
    tq = q_ref.shape[1]
    rows = HEADS * tq
    q = q_ref[...].reshape(rows, 256)
    m_scr[...] = jnp.full_like(m_scr, NEG_INF)
    l_scr[...] = jnp.zeros_like(l_scr)
    acc_scr[...] = jnp.zeros_like(acc_scr)

    def step(kb, masked):
        k = k_ref[pl.ds(pl.multiple_of(kb * tq, tq), tq), :]
        s = _dot_nt(q, k) * MLA_SCALE
        if masked:
            t_idx = lax.broadcasted_iota(jnp.int32, (HEADS, tq, tq), 1).reshape(rows, tq)
            c_idx = lax.broadcasted_iota(jnp.int32, (rows, tq), 1)
            s = jnp.where(c_idx <= t_idx, s, NEG_INF)
        m_prev = m_scr[...]
        m_new = jnp.maximum(m_prev, jnp.max(s, -1, keepdims=True))
        alpha = jnp.exp(m_prev - m_new)
        p = jnp.exp(s - m_new)
        l_scr[...] = alpha * l_scr[...] + jnp.sum(p, -1, keepdims=True)
        acc_scr[...] = alpha * acc_scr[...] + _dot(p.astype(BF16), k[:, 0:128])
        m_scr[...] = m_new

    def body(kb, carry):
        step(kb, False)
        return carry

    lax.fori_loop(0, qi, body, 0)
    step(qi, True)
    o = (acc_scr[...] / l_scr[...]).astype(BF16)
    of = _dot(o, wuv_ref[...])
    lane = lax.broadcasted_iota(jnp.int32, (tq, WIDTH), 1)
    out = jnp.zeros((tq, WIDTH), F32)
    for h in range(HEADS):
        out = jnp.where((lane >= 64 * h) & (lane < 64 * h + 64), of[h * tq:(h + 1) * tq, :], out)
    o_ref[...] = out


def _mla_prompt(q, kcat, wuv, bsz, t_len):
    tq = min(CHUNK, t_len)
    nq = t_len // tq
    n = bsz * t_len
    return pl.pallas_call(
        _flash_kernel,
        grid=(bsz, nq),
        in_specs=[pl.BlockSpec((HEADS, tq, 256), lambda b, i: (0, b * nq + i, 0)),
                  pl.BlockSpec((t_len, 256), lambda b, i: (b, 0)),
                  pl.BlockSpec((128, WIDTH), lambda b, i: (0, 0))],
        out_specs=pl.BlockSpec((tq, WIDTH), lambda b, i: (b * nq + i, 0)),
        out_shape=jax.ShapeDtypeStruct((n, WIDTH), F32),
        scratch_shapes=[pltpu.VMEM((HEADS * tq, 1), F32), pltpu.VMEM((HEADS * tq, 1), F32),
                        pltpu.VMEM((HEADS * tq, 128), F32)],
        compiler_params=_cparams("parallel", "arbitrary"),
        name="mla_prompt",
    )(q, kcat, wuv)


def _decode_kernel(pt_ref, qa_ref, qr_ref, cn_ref, krn_ref, wuv_ref, ckv_hbm, kr_hbm, o_ref,
                   bufc, bufr, sem, *, n_pages, cp):
    b = pl.program_id(0)
    nb = pl.num_programs(0)
    n_chunks = n_pages // cp

    def copies(bb, c, slot, p):
        page = pt_ref[bb * n_pages + c * cp + p]
        return (pltpu.make_async_copy(ckv_hbm.at[page], bufc.at[slot, p], sem.at[0, slot]),
                pltpu.make_async_copy(kr_hbm.at[page], bufr.at[slot, p], sem.at[1, slot]))

    def start_chunk(bb, c, slot):
        for p in range(cp):
            for cpy in copies(bb, c, slot, p):
                cpy.start()

    def wait_chunk(bb, c, slot):
        for p in range(cp):
            for cpy in copies(bb, c, slot, p):
                cpy.wait()

    @pl.when(b == 0)
    def _():
        start_chunk(0, 0, 0)

    qa = qa_ref[0]
    qr = qr_ref[0]
    cn = cn_ref[0].astype(BF16)
    krn = krn_ref[0].astype(BF16)
    s_new = (jnp.sum(qa.astype(F32) * cn.astype(F32), -1, keepdims=True)
             + jnp.sum(qr.astype(F32) * krn.astype(F32), -1, keepdims=True)) * MLA_SCALE
    m0 = s_new
    l0 = jnp.ones((8, 1), F32)
    acc0 = jnp.broadcast_to(cn.astype(F32), (8, 128))

    def body(c, carry):
        m_prev, l_prev, acc = carry
        slot = (b * n_chunks + c) % 2

        @pl.when(c + 1 < n_chunks)
        def _():
            start_chunk(b, c + 1, 1 - slot)

        @pl.when((c + 1 == n_chunks) & (b + 1 < nb))
        def _():
            start_chunk(b + 1, 0, 1 - slot)

        wait_chunk(b, c, slot)
        kc = bufc[slot].reshape(cp * 128, 128).astype(BF16)
        kr = bufr[slot].reshape(cp * 128, C_ROPE).astype(BF16)
        s = (_dot_nt(qa, kc) + _dot_nt(qr, kr)) * MLA_SCALE
        m_new = jnp.maximum(m_prev, jnp.max(s, -1, keepdims=True))
        alpha = jnp.exp(m_prev - m_new)
        p = jnp.exp(s - m_new)
        l_new = alpha * l_prev + jnp.sum(p, -1, keepdims=True)
        acc_new = alpha * acc + _dot(p.astype(BF16), kc)
        return m_new, l_new, acc_new

    _, l_fin, acc_fin = lax.fori_loop(0, n_chunks, body, (m0, l0, acc0))
    o = (acc_fin / l_fin).astype(BF16)
    of = _dot(o, wuv_ref[...])
    row = lax.broadcasted_iota(jnp.int32, (8, WIDTH), 0)
    lane = lax.broadcasted_iota(jnp.int32, (8, WIDTH), 1)
    diag = (lane >= 64 * row) & (lane < 64 * row + 64)
    o_ref[0] = jnp.sum(jnp.where(diag, of, 0.0), axis=0, keepdims=True)


def _mla_decode(page_table, qa8, qr8, c_new, kr_new, wuv, cache_ckv_l, cache_kr_l):
    bsz, n_pages = page_table.shape
    cp = min(PAGES_PER_STEP, n_pages // 2)
    page = cache_ckv_l.shape[1]
    kern = functools.partial(_decode_kernel, n_pages=n_pages, cp=cp)
    grid_spec = pltpu.PrefetchScalarGridSpec(
        num_scalar_prefetch=1,
        grid=(bsz,),
        in_specs=[pl.BlockSpec((1, 8, 128), lambda b, pt: (b, 0, 0)),
                  pl.BlockSpec((1, 8, C_ROPE), lambda b, pt: (b, 0, 0)),
                  pl.BlockSpec((1, 1, 128), lambda b, pt: (b, 0, 0)),
                  pl.BlockSpec((1, 1, C_ROPE), lambda b, pt: (b, 0, 0)),
                  pl.BlockSpec((128, WIDTH), lambda b, pt: (0, 0)),
                  pl.BlockSpec(memory_space=pl.ANY),
                  pl.BlockSpec(memory_space=pl.ANY)],
        out_specs=pl.BlockSpec((1, 1, WIDTH), lambda b, pt: (b, 0, 0)),
        scratch_shapes=[pltpu.VMEM((2, cp, page, 128), F32), pltpu.VMEM((2, cp, page, C_ROPE), F32),
                        pltpu.SemaphoreType.DMA((2, 2))],
    )
    return pl.pallas_call(
        kern,
        grid_spec=grid_spec,
        out_shape=jax.ShapeDtypeStruct((bsz, 1, WIDTH), F32),
        compiler_params=_cparams("arbitrary"),
        name="mla_decode",
    )(page_table.reshape(-1), qa8, qr8, c_new, kr_new, wuv, cache_ckv_l, cache_kr_l)


def _mlstm_step_kernel(qr_ref, kr_ref, vr_ref, or_ref, qc_ref, kc_ref, ig_ref, fg_ref, c_ref, n_ref, m_ref,
                       bi_ref, bf_ref, norm_ref, out_ref, c1_ref, n1_ref, m1_ref):
    q = qr_ref[...]
    k = kr_ref[...] * 0.125
    ig = ig_ref[...] + bi_ref[...]
    lf = _log_sigmoid(fg_ref[...] + bf_ref[...])
    m0 = m_ref[...]
    c0 = c_ref[...]
    n0 = n_ref[...]
    inter = lf + m0
    mt = jnp.maximum(ig, inter)
    s = jnp.exp(ig - mt) * jnp.sum(q * k, -1, keepdims=True)
    g = jnp.exp(inter - mt)
    qc = jnp.sum(qc_ref[...] * c0, axis=2, keepdims=True)
    num = s * vr_ref[...] + g * qc
    den = s + g * jnp.sum(q * n0, -1, keepdims=True)
    hh = num / jnp.maximum(jnp.abs(den), jnp.exp(-mt))
    mu = jnp.mean(hh, -1, keepdims=True)
    dv = hh - mu
    var = jnp.mean(dv * dv, -1, keepdims=True)
    hn = dv * lax.rsqrt(var + EPS) * norm_ref[...]
    out_ref[...] = jax.nn.sigmoid(or_ref[...]) * hn
    w = jnp.exp(ig - mt)
    decay = jnp.exp(inter - mt)
    c1_ref[...] = decay * c0 + (w * 0.125 * kc_ref[...]) * vr_ref[...]
    n1_ref[...] = decay * n0 + w * k
    m1_ref[...] = mt


def _mlstm_step(q, k, v, o, ig, fg, c0, n0, m0, bi, bf, norm):
    bsz = q.shape[0]
    bb = min(16, bsz)
    r4 = lambda a: a.reshape(bsz, HEADS, 1, DH)
    c4 = lambda a: a.reshape(bsz, HEADS, DH, 1)
    s4 = lambda a: a.reshape(bsz, HEADS, 1, 1)
    blk = lambda d2, d3: pl.BlockSpec((bb, HEADS, d2, d3), lambda i: (i, 0, 0, 0))
    par = lambda d2, d3: pl.BlockSpec((1, HEADS, d2, d3), lambda i: (0, 0, 0, 0))
    return pl.pallas_call(
        _mlstm_step_kernel,
        grid=(bsz // bb,),
        in_specs=[blk(1, DH)] * 4 + [blk(DH, 1)] * 2 + [blk(1, 1)] * 2
                 + [blk(DH, DH), blk(1, DH), blk(1, 1), par(1, 1), par(1, 1), par(1, DH)],
        out_specs=[blk(1, DH), blk(DH, DH), blk(1, DH), blk(1, 1)],
        out_shape=[jax.ShapeDtypeStruct((bsz, HEADS, 1, DH), F32),
                   jax.ShapeDtypeStruct((bsz, HEADS, DH, DH), F32),
                   jax.ShapeDtypeStruct((bsz, HEADS, 1, DH), F32),
                   jax.ShapeDtypeStruct((bsz, HEADS, 1, 1), F32)],
        compiler_params=_cparams("parallel"),
        name="mlstm_step",
    )(r4(q), r4(k), r4(v), r4(o), c4(q), c4(k), s4(ig), s4(fg), c0, r4(n0), s4(m0),
      bi.reshape(1, HEADS, 1, 1), bf.reshape(1, HEADS, 1, 1), norm.reshape(1, HEADS, 1, DH))


def _pool_step_kernel(xb_ref, st_ref, pooled_ref, st1_ref, *, pos):
    xb = xb_ref[...]
    lane = lax.broadcasted_iota(jnp.int32, xb.shape, 2)
    acc = xb
    pooled = jnp.zeros_like(xb)
    for k in range(1, 16):
        acc = acc + st_ref[:, POOL_BUF - k:POOL_BUF - k + 1, :]
        if k + 1 in POOL_WINDOWS:
            g = POOL_WINDOWS.index(k + 1)
            cnt = float(min(pos + 1, k + 1))
            pooled = jnp.where((lane >= 64 * g) & (lane < 64 * g + 64), acc / cnt - xb, pooled)
    pooled_ref[...] = pooled
    st1_ref[:, 0:POOL_BUF - 1, :] = st_ref[:, 1:POOL_BUF, :]
    st1_ref[:, POOL_BUF - 1:POOL_BUF, :] = xb


def _pool_step(xb, state, pos):
    bsz = xb.shape[0]
    bb = min(32, bsz)
    return pl.pallas_call(
        functools.partial(_pool_step_kernel, pos=pos),
        grid=(bsz // bb,),
        in_specs=[pl.BlockSpec((bb, 1, WIDTH), lambda i: (i, 0, 0)),
                  pl.BlockSpec((bb, POOL_BUF, WIDTH), lambda i: (i, 0, 0))],
        out_specs=[pl.BlockSpec((bb, 1, WIDTH), lambda i: (i, 0, 0)),
                   pl.BlockSpec((bb, POOL_BUF, WIDTH), lambda i: (i, 0, 0))],
        out_shape=[jax.ShapeDtypeStruct((bsz, 1, WIDTH), F32),
                   jax.ShapeDtypeStruct((bsz, POOL_BUF, WIDTH), F32)],
        compiler_params=_cparams("parallel"),
        name="pool_step",
    )(xb.reshape(bsz, 1, WIDTH), state)


def _pool_lin_kernel(p_ref, w_ref, scale_ref, o_ref):
    o_ref[...] = _dot(p_ref[...].astype(BF16), w_ref[...]) * scale_ref[...]


def _pool_lin(pooled, w_blk, scale_row):
    n = pooled.shape[0]
    return pl.pallas_call(
        _pool_lin_kernel,
        out_shape=jax.ShapeDtypeStruct((n, WIDTH), F32),
        name="pool_lin",
    )(pooled, w_blk, scale_row)


def _ssd_step_kernel(x0_ref, x1_ref, x2_ref, x3_ref, bc_ref, st_ref, full_ref, dz_ref, ddt_ref, h_ref,
                     cwx_ref, cbx_ref, cwbc_ref, cbbc_ref, dtb_ref, alog_ref, skip_ref, norm_ref,
                     out_ref, st1_ref, h1_ref):
    xc = cbx_ref[...] + x0_ref[...] * cwx_ref[0] + x1_ref[...] * cwx_ref[1] \
        + x2_ref[...] * cwx_ref[2] + x3_ref[...] * cwx_ref[3]
    xc = _silu(xc)
    bc = cbbc_ref[...] + bc_ref[...] * cwbc_ref[3:4, :]
    for tap in range(D_CONV - 1):
        bc = bc + st_ref[:, tap:tap + 1, 256:768] * cwbc_ref[tap:tap + 1, :]
    bc = _silu(bc)
    dt = _softplus(ddt_ref[...] + dtb_ref[...])
    dec = jnp.exp(dt * (-jnp.exp(alog_ref[...])))
    xdt = xc * dt
    ys = []
    for g in range(2):
        bg = bc[:, :, 128 * g:128 * g + 128][:, None]
        cg = bc[:, :, 256 + 128 * g:256 + 128 * g + 128][:, None]
        h1 = h_ref[:, 2 * g:2 * g + 2] * dec[:, 2 * g:2 * g + 2] + xdt[:, 2 * g:2 * g + 2] * bg
        h1_ref[:, 2 * g:2 * g + 2] = h1
        ys.append(jnp.sum(h1 * cg, -1, keepdims=True))
    for g in range(2):
        y = ys[g] + skip_ref[:, 2 * g:2 * g + 2] * xc[:, 2 * g:2 * g + 2]
        y = y * _silu(dz_ref[:, 2 * g:2 * g + 2])
        ss = jnp.sum(y * y, axis=2, keepdims=True)
        ms = (ss[:, 0:1] + ss[:, 1:2]) * (1.0 / 128)
        out_ref[:, 2 * g:2 * g + 2] = y * lax.rsqrt(ms + EPS) * norm_ref[:, 2 * g:2 * g + 2]
    st1_ref[:, 0:D_CONV - 2, :] = st_ref[:, 1:D_CONV - 1, :]
    st1_ref[:, D_CONV - 2:D_CONV - 1, :] = full_ref[...]


def _ssd_step(dxbc, dz, ddt, state_conv, h0, cw, cb, dt_bias, a_log, skip, norm):
    bsz = dxbc.shape[0]
    bb = min(8, bsz)
    c4 = lambda a: a.reshape(bsz, HEADS, DH, 1)
    b4 = lambda d1, d2, d3: pl.BlockSpec((bb, d1, d2, d3), lambda i: (i, 0, 0, 0))
    b3 = lambda d1, d2: pl.BlockSpec((bb, d1, d2), lambda i: (i, 0, 0))
    par = lambda shape: pl.BlockSpec(shape, lambda i: (0,) * len(shape))
    col = b4(HEADS, DH, 1)
    return pl.pallas_call(
        _ssd_step_kernel,
        grid=(bsz // bb,),
        in_specs=[col, col, col, col, b3(1, 512), b3(D_CONV - 1, D_CONV_DIM), b3(1, D_CONV_DIM), col,
                  b4(HEADS, 1, 1), b4(HEADS, DH, D_STATE),
                  par((D_CONV, HEADS, DH, 1)), par((1, HEADS, DH, 1)), par((D_CONV, 512)), par((1, 512)),
                  par((1, HEADS, 1, 1)), par((1, HEADS, 1, 1)), par((1, HEADS, 1, 1)), par((1, HEADS, DH, 1))],
        out_specs=[col, b3(D_CONV - 1, D_CONV_DIM), b4(HEADS, DH, D_STATE)],
        out_shape=[jax.ShapeDtypeStruct((bsz, HEADS, DH, 1), F32),
                   jax.ShapeDtypeStruct((bsz, D_CONV - 1, D_CONV_DIM), F32),
                   jax.ShapeDtypeStruct((bsz, HEADS, DH, D_STATE), F32)],
        compiler_params=_cparams("parallel"),
        name="ssd_step",
    )(c4(state_conv[:, 0, :256]), c4(state_conv[:, 1, :256]), c4(state_conv[:, 2, :256]), c4(dxbc[:, :256]),
      dxbc[:, 256:].reshape(bsz, 1, 512), state_conv, dxbc.reshape(bsz, 1, D_CONV_DIM), c4(dz),
      ddt.reshape(bsz, HEADS, 1, 1), h0,
      cw[:, :256].reshape(D_CONV, HEADS, DH, 1), cb[:256].reshape(1, HEADS, DH, 1), cw[:, 256:],
      cb[256:].reshape(1, 512), dt_bias.reshape(1, HEADS, 1, 1), a_log.reshape(1, HEADS, 1, 1),
      skip.reshape(1, HEADS, 1, 1), norm.reshape(1, HEADS, DH, 1))


def _mla_prep_s_kernel(cq_ref, ckv_ref, k1_ref, k2_ref, csa_ref, csb_ref, qn_ref, kvn_ref, wn_ref, wxa_ref,
                       wxb_ref, wuk_ref, clat_ref, kr_ref, qa_ref, qr_ref):
    ql = _rms(cq_ref[...], qn_ref[...]).astype(BF16)
    qabs = _dot(_dot(ql, wn_ref[...]).astype(BF16), wuk_ref[...])
    csa = csa_ref[...]
    csb = csb_ref[...]
    for h in range(HEADS):
        qa_ref[h] = qabs[:, 128 * h:128 * h + 128].astype(BF16)
        qr_ref[h] = (_dot(ql, wxa_ref[h]) * csa + _dot(ql, wxb_ref[h]) * csb).astype(BF16)
    clat_ref[...] = _rms(ckv_ref[...], kvn_ref[...])
    kr_ref[...] = k1_ref[:, 0:C_ROPE] * csa + k2_ref[:, 0:C_ROPE] * csb


def _mla_prep_sample(u, csa, csb, qn, kvn, wn, wxa, wxb, wuk):
    n = u.shape[0]
    const = lambda shape: pl.BlockSpec(shape, lambda i: (0,) * len(shape))
    col = lambda w, off: pl.BlockSpec((n, w), lambda i: (0, off // w))
    return pl.pallas_call(
        _mla_prep_s_kernel,
        grid=(1,),
        in_specs=[col(256, U_CQ), col(128, U_CKV), col(128, U_K1), col(128, U_K2),
                  const((1, C_ROPE)), const((1, C_ROPE)), const((1, 256)), const((1, 128)), const((256, 256)),
                  const((HEADS, 256, C_ROPE)), const((HEADS, 256, C_ROPE)), const((256, 512))],
        out_specs=[const((n, 128)), const((n, C_ROPE)), const((HEADS, n, 128)), const((HEADS, n, C_ROPE))],
        out_shape=[jax.ShapeDtypeStruct((n, 128), F32),
                   jax.ShapeDtypeStruct((n, C_ROPE), F32),
                   jax.ShapeDtypeStruct((HEADS, n, 128), BF16),
                   jax.ShapeDtypeStruct((HEADS, n, C_ROPE), BF16)],
        compiler_params=_cparams("arbitrary"),
        name="mla_prep_sample",
    )(u, u, u, u, csa, csb, qn, kvn, wn, wxa, wxb, wuk)


def _merge_kernel(x_ref, oa_ref, ob_ref, oc_ref, od_ref, g0_ref, g1_ref, g2_ref, g3_ref, wb_ref, wo_ref,
                  gn_ref, out_ref):
    mixed = None
    for i, (o_ref, g_ref) in enumerate(((oa_ref, g0_ref), (ob_ref, g1_ref), (oc_ref, g2_ref), (od_ref, g3_ref))):
        term = jax.nn.sigmoid(g_ref[...]) * _dot(o_ref[...].astype(BF16), wb_ref[i])
        mixed = term if mixed is None else mixed + term
    y = _dot(mixed.astype(BF16), wo_ref[...])
    out_ref[...] = x_ref[...] + _rms(y, gn_ref[...])


def _merge(x, outs, u, wb, wo, gn):
    n, d = x.shape
    tm = min(512, n)
    row = lambda w: pl.BlockSpec((tm, w), lambda i: (i, 0))
    gate = lambda k: pl.BlockSpec((tm, d), lambda i: (i, U_GATES // d + k))
    return pl.pallas_call(
        _merge_kernel,
        grid=(n // tm,),
        in_specs=[row(d)] + [row(WIDTH)] * 4 + [gate(k) for k in range(4)]
                 + [pl.BlockSpec((4, WIDTH, d), lambda i: (0, 0, 0)), pl.BlockSpec((d, d), lambda i: (0, 0)),
                    pl.BlockSpec((1, d), lambda i: (0, 0))],
        out_specs=row(d),
        out_shape=jax.ShapeDtypeStruct((n, d), F32),
        compiler_params=_cparams("parallel"),
        name="merge",
    )(x, *outs, u, u, u, u, wb, wo, gn)


def _ffn_kernel(x_ref, g1_ref, wg_ref, wu_ref, wd_ref, g2_ref, out_ref, xn_scr, acc_scr):
    k = pl.program_id(1)

    @pl.when(k == 0)
    def _():
        xn_scr[...] = _rms(x_ref[...], g1_ref[...]).astype(BF16)
        acc_scr[...] = jnp.zeros_like(acc_scr)

    xn = xn_scr[...]
    hact = _silu(_dot(xn, wg_ref[...])) * _dot(xn, wu_ref[...])
    acc_scr[...] += _dot(hact.astype(BF16), wd_ref[...])

    @pl.when(k == pl.num_programs(1) - 1)
    def _():
        out_ref[...] = x_ref[...] + _rms(acc_scr[...], g2_ref[...])


def _ffn(x, g1, wgu, wd, g2):
    n, d = x.shape
    hid = wd.shape[0]
    tm = min(512, n)
    nk = 2
    th = hid // nk
    return pl.pallas_call(
        _ffn_kernel,
        grid=(n // tm, nk),
        in_specs=[pl.BlockSpec((tm, d), lambda i, k: (i, 0)),
                  pl.BlockSpec((1, d), lambda i, k: (0, 0)),
                  pl.BlockSpec((d, th), lambda i, k: (0, k)),
                  pl.BlockSpec((d, th), lambda i, k: (0, k + nk)),
                  pl.BlockSpec((th, d), lambda i, k: (k, 0)),
                  pl.BlockSpec((1, d), lambda i, k: (0, 0))],
        out_specs=pl.BlockSpec((tm, d), lambda i, k: (i, 0)),
        out_shape=jax.ShapeDtypeStruct((n, d), F32),
        scratch_shapes=[pltpu.VMEM((tm, d), BF16), pltpu.VMEM((tm, d), F32)],
        compiler_params=_cparams("parallel", "arbitrary"),
        name="ffn",
    )(x, g1, wgu, wgu, wd, g2)


def _lane_row(width, pieces):
    row = jnp.zeros((1, width), F32)
    for off, vals in pieces:
        row = row.at[0, off:off + vals.shape[0]].set(vals.astype(F32))
    return row


def _layer_params(l, n_mix_pre, n_mix_post, n_ffn_pre, n_ffn_post, w_in, a_bi, a_bf, a_norm, b_w, b_scale,
                  c_qnorm, c_wuq, c_kvnorm, c_wuk, c_wuv, d_conv_w, d_conv_b, d_dt_bias, d_a_log, d_skip, d_norm,
                  w_branch, w_o, w_gu, w_down):
    offs = np.concatenate([[0], np.cumsum(IN_SPLITS)]).tolist()
    seg = [w_in[l][:, offs[i]:offs[i + 1]] for i in range(len(IN_SPLITS))]
    aq, ak, av, ao, ai, af, xb, cq, ckv, ckr, dz, dxbc, ddt, gates = seg
    z = lambda w: jnp.zeros((D_MODEL, w), F32)
    small = jnp.concatenate([ckr, ai, z(4), af, z(4), ddt, z(128 - SM_DT - 4)], 1)
    k1 = jnp.tile(ckr[:, :16], (1, 8))
    k2 = jnp.tile(ckr[:, 16:], (1, 8))
    p = {}
    p["w_in"] = jnp.concatenate([aq, ak, av, ao, xb, cq, dxbc, dz, ckv, small, k1, k2, gates], 1).astype(BF16)
    p["n_mix_pre"] = n_mix_pre[l][None]
    p["n_mix_post"] = n_mix_post[l][None]
    p["n_ffn_pre"] = n_ffn_pre[l][None]
    p["n_ffn_post"] = n_ffn_post[l][None]
    p["a_bias_row"] = _lane_row(128, [(SM_IG, a_bi[l]), (SM_FG, a_bf[l])])
    p["a_norm_row"] = a_norm[l][None]
    p["a_bi"], p["a_bf"], p["a_norm"] = a_bi[l], a_bf[l], a_norm[l]
    wblk = jnp.zeros((WIDTH, WIDTH), F32)
    for g in range(4):
        wblk = wblk.at[64 * g:64 * g + 64, 64 * g:64 * g + 64].set(b_w[l][g])
    p["b_wblk"] = wblk.astype(BF16)
    p["b_scale_row"] = b_scale[l][None]
    wuq = c_wuq[l].reshape(256, HEADS, 96)
    p["c_wn"] = wuq[:, :, :64].reshape(256, 256).astype(BF16)
    wx1 = wuq[:, :, 64:80]
    wx2 = wuq[:, :, 80:96]
    p["c_wx1"] = jnp.tile(wx1.reshape(256, 64), (1, 2)).astype(BF16)
    p["c_wx2"] = jnp.tile(wx2.reshape(256, 64), (1, 2)).astype(BF16)
    p["c_wxa"] = jnp.concatenate([wx1, wx1], -1).transpose(1, 0, 2).astype(BF16)
    p["c_wxb"] = jnp.concatenate([wx2, wx2], -1).transpose(1, 0, 2).astype(BF16)
    wuk_bd = jnp.zeros((256, 512), F32)
    for h in range(HEADS):
        wuk_bd = wuk_bd.at[64 * h:64 * h + 64, 128 * h:128 * h + 128].set(c_wuk[l][:, h, :].T)
    p["c_wuk_bd"] = wuk_bd.astype(BF16)
    p["c_wuv"] = c_wuv[l].reshape(C_KV_RANK, WIDTH).astype(BF16)
    p["c_qnorm"] = c_qnorm[l][None]
    p["c_kvnorm"] = c_kvnorm[l][None]
    p["d_conv_w"] = d_conv_w[l]
    p["d_conv_b"] = d_conv_b[l]
    p["d_conv_b_row"] = d_conv_b[l][None]
    p["d_bias_row"] = _lane_row(128, [(SM_DT, d_dt_bias[l])])
    p["d_alog_row"] = _lane_row(128, [(SM_DT, d_a_log[l])])
    p["d_alog_col"] = _lane_row(8, [(0, d_a_log[l])]).reshape(8, 1)
    p["d_skip_row"] = jnp.repeat(d_skip[l], DH)[None]
    p["d_norm_row"] = d_norm[l][None]
    p["d_dt_bias"], p["d_a_log"], p["d_skip"], p["d_norm"] = d_dt_bias[l], d_a_log[l], d_skip[l], d_norm[l]
    p["w_branch"] = w_branch[l].astype(BF16)
    p["w_o"] = w_o[l].astype(BF16)
    p["w_gu"] = w_gu[l].astype(BF16)
    p["w_down"] = w_down[l].astype(BF16)
    return p


def _rope_tables(pos):
    inv = ROPE_THETA ** (-jnp.arange(16, dtype=F32) / 16)
    ang = pos.astype(F32)[:, None] * inv[None, :]
    cos, sin = jnp.cos(ang), jnp.sin(ang)
    cos64, sin64 = jnp.tile(cos, (1, 4)), jnp.tile(sin, (1, 4))
    cs1 = jnp.concatenate([cos64, sin64], 1)
    cs2 = jnp.concatenate([-sin64, cos64], 1)
    csa = jnp.concatenate([cos, sin], 1)
    csb = jnp.concatenate([-sin, cos], 1)
    return cs1, cs2, csa, csb


def _head_lane_mask():
    lane = np.arange(128)
    return jnp.asarray(((lane % 64) // 16)[None, :] == np.arange(HEADS)[:, None], F32)


def _prompt_layer(x, bsz, t_len, p, tabs, hm):
    u = _in_proj(x, p["n_mix_pre"], p["w_in"])
    out_a, c_pad, m_pad = _mlstm_prompt(u, bsz, t_len, p["a_bias_row"], p["a_norm_row"])
    out_b, pool1 = _pool_prompt(u, bsz, t_len, p["b_wblk"], p["b_scale_row"])
    out_d, conv1, ssm1 = _ssd_prompt(u, bsz, t_len, p["d_conv_w"], p["d_conv_b_row"], p["d_bias_row"],
                                     p["d_alog_row"], p["d_alog_col"], p["d_skip_row"], p["d_norm_row"])
    c_lat, k_rope, kcat, q = _mla_prep(u, t_len, tabs, p["c_qnorm"], p["c_kvnorm"], p["c_wn"], p["c_wx1"],
                                       p["c_wx2"], p["c_wuk_bd"], hm)
    out_c = _mla_prompt(q, kcat, p["c_wuv"], bsz, t_len)
    x = _merge(x, (out_a, out_b, out_c, out_d), u, p["w_branch"], p["w_o"], p["n_mix_post"])
    x = _ffn(x, p["n_ffn_pre"], p["w_gu"], p["w_down"], p["n_ffn_post"])
    c1 = jnp.stack([c_pad[:, h, 64 * (h % 2):64 * (h % 2) + 64, :64] for h in range(HEADS)], 1)
    n1 = jnp.stack([c_pad[:, h, 64 * (h % 2):64 * (h % 2) + 64, 64] for h in range(HEADS)], 1)
    m1 = m_pad[:, :HEADS, 0]
    state = (c1, n1, m1, pool1, conv1, ssm1.reshape(bsz, HEADS, DH, D_STATE))
    rows = (c_lat.reshape(bsz, t_len, C_KV_RANK), k_rope.reshape(bsz, t_len, C_ROPE))
    return x, state, rows


def _sample_layer(x, p, tabs_s, state, page_table, cache_ckv_l, cache_kr_l, past_len):
    c0, n0, m0, pool0, conv0, ssm0 = state
    bsz = x.shape[0]
    u = _in_proj(x, p["n_mix_pre"], p["w_in"])
    sm = u[:, U_SMALL:U_SMALL + 128]
    out_a, c1, n1, m1 = _mlstm_step(u[:, 0:256], u[:, 256:512], u[:, 512:768], u[:, 768:1024],
                                    sm[:, SM_IG:SM_IG + 4], sm[:, SM_FG:SM_FG + 4], c0, n0, m0,
                                    p["a_bi"], p["a_bf"], p["a_norm"])
    pooled, pool1 = _pool_step(u[:, U_XB:U_XB + WIDTH], pool0, past_len)
    out_b = _pool_lin(pooled.reshape(bsz, WIDTH), p["b_wblk"], p["b_scale_row"])
    out_d, conv1, ssm1 = _ssd_step(u[:, U_DXBC:U_DXBC + D_CONV_DIM], u[:, U_DZ:U_DZ + WIDTH],
                                   sm[:, SM_DT:SM_DT + 4], conv0, ssm0, p["d_conv_w"], p["d_conv_b"],
                                   p["d_dt_bias"], p["d_a_log"], p["d_skip"], p["d_norm"])
    c_lat, k_rope, qa, qr = _mla_prep_sample(u, tabs_s[2], tabs_s[3], p["c_qnorm"], p["c_kvnorm"], p["c_wn"],
                                             p["c_wxa"], p["c_wxb"], p["c_wuk_bd"])
    pad8 = lambda a: jnp.pad(a.transpose(1, 0, 2), ((0, 0), (0, 8 - HEADS), (0, 0)))
    out_c = _mla_decode(page_table, pad8(qa), pad8(qr), c_lat.reshape(bsz, 1, C_KV_RANK),
                        k_rope.reshape(bsz, 1, C_ROPE), p["c_wuv"], cache_ckv_l, cache_kr_l)
    outs = (out_a.reshape(bsz, WIDTH), out_b, out_c.reshape(bsz, WIDTH), out_d.reshape(bsz, WIDTH))
    x = _merge(x, outs, u, p["w_branch"], p["w_o"], p["n_mix_post"])
    x = _ffn(x, p["n_ffn_pre"], p["w_gu"], p["w_down"], p["n_ffn_post"])
    new_state = (c1, n1.reshape(bsz, HEADS, DH), m1.reshape(bsz, HEADS), pool1, conv1, ssm1)
    rows = (c_lat.reshape(bsz, 1, C_KV_RANK), k_rope.reshape(bsz, 1, C_ROPE))
    return x, new_state, rows


def kernel(x_prompt, x_sample, state_mlstm_c, state_mlstm_n, state_mlstm_m, state_pool, cache_ckv, cache_krope,
           page_table, state_conv, state_ssm, n_mix_pre, n_mix_post, n_ffn_pre, n_ffn_post, w_in, a_bi, a_bf,
           a_norm, b_w, b_scale, c_qnorm, c_wuq, c_kvnorm, c_wuk, c_wuv, d_conv_w, d_conv_b, d_dt_bias, d_a_log,
           d_skip, d_norm, w_branch, w_o, w_gu, w_down):
    bp, t_len, d = x_prompt.shape
    bs = x_sample.shape[0]
    assert x_sample.shape[1] == 1 and d == D_MODEL
    depth = w_in.shape[0]
    past_len = page_table.shape[1] * cache_ckv.shape[2]
    tabs_p = _rope_tables(jnp.arange(t_len))
    tabs_s = _rope_tables(jnp.full((1,), past_len))
    hm = _head_lane_mask()
    yp = x_prompt.reshape(bp * t_len, d)
    ys = x_sample.reshape(bs, d)
    st_p, st_s, rows_p, rows_s = [], [], [], []
    for l in range(depth):
        p = _layer_params(l, n_mix_pre, n_mix_post, n_ffn_pre, n_ffn_post, w_in, a_bi, a_bf, a_norm, b_w, b_scale,
                          c_qnorm, c_wuq, c_kvnorm, c_wuk, c_wuv, d_conv_w, d_conv_b, d_dt_bias, d_a_log, d_skip,
                          d_norm, w_branch, w_o, w_gu, w_down)
        yp, sp, rp = _prompt_layer(yp, bp, t_len, p, tabs_p, hm)
        in_state = (state_mlstm_c[l], state_mlstm_n[l], state_mlstm_m[l], state_pool[l], state_conv[l], state_ssm[l])
        ys, ss, rs = _sample_layer(ys, p, tabs_s, in_state, page_table, cache_ckv[l], cache_krope[l], past_len)
        st_p.append(sp)
        st_s.append(ss)
        rows_p.append(rp)
        rows_s.append(rs)

    stk = lambda lst, i: jnp.stack([s[i] for s in lst], 0)
    return (yp.reshape(bp, t_len, d), ys.reshape(bs, 1, d),
            stk(rows_p, 0), stk(rows_p, 1), stk(rows_s, 0), stk(rows_s, 1),
            stk(st_p, 0), stk(st_p, 1), stk(st_p, 2), stk(st_s, 0), stk(st_s, 1), stk(st_s, 2),
            stk(st_p, 3), stk(st_s, 3), stk(st_p, 4), stk(st_s, 4), stk(st_p, 5), stk(st_s, 5))
```

```python
import functools
import math

import jax
import jax.numpy as jnp
import numpy as np
from jax import lax
from jax.experimental import pallas as pl
from jax.experimental.pallas import tpu as pltpu

F32 = jnp.float32
BF16 = jnp.bfloat16
HIGHEST = lax.Precision.HIGHEST
NEG_INF = float("-inf")

EPS = 1e-6
D_MODEL = 1024
HEADS = 4
DH = 64
WIDTH = 256
C_ROPE = 32
C_KV_RANK = 128
ROPE_THETA = 10000.0
MLA_SCALE = (64 + C_ROPE) ** -0.5
D_STATE = 128
D_CONV = 4
D_CONV_DIM = 768
POOL_WINDOWS = (2, 4, 8, 16)
POOL_BUF = 15
FFN_HIDDEN = 2816
IN_SPLITS = (256, 256, 256, 256, 4, 4, 256, 256, 128, 32, 256, 768, 4, 4096)

U_MLSTM = 0
U_XB = 1024
U_CQ = 1280
U_DXBC = 1536
U_DZ = 2304
U_CKV = 2560
U_SMALL = 2688
U_K1 = 2816
U_K2 = 2944
U_GATES = 3072
U_WIDTH = 7168
SM_IG, SM_FG, SM_DT = 32, 40, 48

VMEM_LIMIT = 56 * 1024 * 1024
CHUNK = 256
PAGES_PER_STEP = 16


def _cparams(*sem):
    return pltpu.CompilerParams(dimension_semantics=sem, vmem_limit_bytes=VMEM_LIMIT)


def _rms(xf, g):
    return xf * lax.rsqrt(jnp.mean(xf * xf, -1, keepdims=True) + EPS) * g


def _softplus(x):
    return jnp.maximum(x, 0.0) + jnp.log1p(jnp.exp(-jnp.abs(x)))


def _log_sigmoid(x):
    return -_softplus(-x)


def _silu(x):
    return x * jax.nn.sigmoid(x)


def _dot(a, b):
    return jnp.dot(a, b, preferred_element_type=F32)


def _dot_nt(a, b):
    return lax.dot_general(a, b, (((1,), (1,)), ((), ())), preferred_element_type=F32)


def _dot_exact(a, b):
    return jnp.dot(a, b, precision=HIGHEST, preferred_element_type=F32)


def _tri(n):
    row = lax.broadcasted_iota(jnp.int32, (n, n), 0)
    col = lax.broadcasted_iota(jnp.int32, (n, n), 1)
    return row >= col, row <= col


def _in_proj_kernel(x_ref, g_ref, w_ref, o_ref, xn_ref):
    @pl.when(pl.program_id(1) == 0)
    def _():
        xn_ref[...] = _rms(x_ref[...], g_ref[...]).astype(BF16)

    o_ref[...] = _dot(xn_ref[...], w_ref[...])


def _in_proj(x, g, w):
    n, d = x.shape
    wn = w.shape[1]
    tm = min(n, 1024)
    tn = 1792
    return pl.pallas_call(
        _in_proj_kernel,
        grid=(n // tm, wn // tn),
        in_specs=[pl.BlockSpec((tm, d), lambda i, j: (i, 0)),
                  pl.BlockSpec((1, d), lambda i, j: (0, 0)),
                  pl.BlockSpec((d, tn), lambda i, j: (0, j))],
        out_specs=pl.BlockSpec((tm, tn), lambda i, j: (i, j)),
        out_shape=jax.ShapeDtypeStruct((n, wn), F32),
        scratch_shapes=[pltpu.VMEM((tm, d), BF16)],
        compiler_params=_cparams("parallel", "arbitrary"),
        name="in_proj",
    )(x, g, w)


def _mlstm_kernel(qkvo_ref, sm_ref, bias_ref, norm_ref, out_ref, cst_ref, mst_ref, c_scr, m_scr):
    j = pl.program_id(1)
    L = qkvo_ref.shape[0]

    @pl.when(j == 0)
    def _():
        c_scr[...] = jnp.zeros_like(c_scr)
        m_scr[...] = jnp.zeros_like(m_scr)

    pre = sm_ref[...] + bias_ref[...]
    causal, anti = _tri(L)
    tril = causal.astype(F32)
    triu = anti.astype(F32)
    bcum = _dot_exact(tril, _log_sigmoid(pre))
    pre_t = pre.T
    ig_t = pre_t[SM_IG:SM_IG + 8, :]
    b_t = _dot_exact(_log_sigmoid(pre_t[SM_FG:SM_FG + 8, :]), triu)
    k_t = qkvo_ref[:, 256:512].T
    lane = lax.broadcasted_iota(jnp.int32, (L, 128), 1)
    sub = lax.broadcasted_iota(jnp.int32, (128, L), 0)

    for pair in range(2):
        hn_pair = jnp.zeros((L, 128), F32)
        for half in range(2):
            h = 2 * pair + half
            lo = 64 * half
            qq = qkvo_ref[:, 128 * pair:128 * pair + 128]
            kk = qkvo_ref[:, 256 + 128 * pair:256 + 128 * pair + 128]
            vv = qkvo_ref[:, 512 + 128 * pair:512 + 128 * pair + 128]
            inhalf = (lane >= lo) & (lane < lo + 64)
            qm = jnp.where(inhalf, qq, 0.0).astype(BF16)
            sc = _dot_nt(qm, kk.astype(BF16)) * 0.125
            bcol = bcum[:, SM_FG + h:SM_FG + h + 1]
            brow = b_t[h:h + 1, :]
            igrow = ig_t[h:h + 1, :]
            dlog = jnp.where(causal, bcol - brow + igrow, NEG_INF)
            m_h = m_scr[h:h + 1, 0:1]
            inter = bcol + m_h
            mt = jnp.maximum(jnp.max(dlog, -1, keepdims=True), inter)
            s = jnp.exp(dlog - mt) * sc
            g = jnp.exp(inter - mt)
            vsh = vv if half == 0 else pltpu.roll(vv, 64, 1)
            vaug = jnp.where(lane < 64, vsh, jnp.where(lane == 64, 1.0, 0.0)).astype(BF16)
            cp = c_scr[h]
            nd = _dot(s.astype(BF16), vaug) + g * _dot(qm, cp.astype(BF16))
            den = nd[:, 64:65]
            hh = nd / jnp.maximum(jnp.abs(den), jnp.exp(-mt))
            valid = lane < 64
            mu = jnp.sum(jnp.where(valid, hh, 0.0), -1, keepdims=True) * (1.0 / 64)
            dv = jnp.where(valid, hh - mu, 0.0)
            var = jnp.sum(dv * dv, -1, keepdims=True) * (1.0 / 64)
            hn = dv * lax.rsqrt(var + EPS)
            hn_pair = hn_pair + (hn if half == 0 else pltpu.roll(hn, 64, 1))
            bl = bcol[L - 1:L, :]
            wlog = bl - brow + igrow
            m_new = jnp.maximum(bl + m_h, jnp.max(wlog, -1, keepdims=True))
            w = jnp.exp(wlog - m_new)
            decay = jnp.exp(bl + m_h - m_new)
            inrows = (sub >= lo) & (sub < lo + 64)
            ktw = jnp.where(inrows, k_t[128 * pair:128 * pair + 128, :], 0.0) * (w * 0.125)
            c_scr[h] = decay * cp + _dot(ktw.astype(BF16), vaug)
            m_scr[h:h + 1, :] = jnp.broadcast_to(m_new, (1, 128))
        og = qkvo_ref[:, 768 + 128 * pair:768 + 128 * pair + 128]
        out_ref[:, 128 * pair:128 * pair + 128] = (
            jax.nn.sigmoid(og) * (hn_pair * norm_ref[:, 128 * pair:128 * pair + 128]))

    @pl.when(j == pl.num_programs(1) - 1)
    def _():
        cst_ref[0] = c_scr[...]
        mst_ref[0] = m_scr[...]


def _mlstm_prompt(u, bsz, t_len, bias_row, norm_row):
    L = min(CHUNK, t_len)
    nc = t_len // L
    n = bsz * t_len
    return pl.pallas_call(
        _mlstm_kernel,
        grid=(bsz, nc),
        in_specs=[pl.BlockSpec((L, 1024), lambda b, j: (b * nc + j, U_MLSTM // 1024)),
                  pl.BlockSpec((L, 128), lambda b, j: (b * nc + j, U_SMALL // 128)),
                  pl.BlockSpec((1, 128), lambda b, j: (0, 0)),
                  pl.BlockSpec((1, WIDTH), lambda b, j: (0, 0))],
        out_specs=[pl.BlockSpec((L, WIDTH), lambda b, j: (b * nc + j, 0)),
                   pl.BlockSpec((1, HEADS, 128, 128), lambda b, j: (b, 0, 0, 0)),
                   pl.BlockSpec((1, 8, 128), lambda b, j: (b, 0, 0))],
        out_shape=[jax.ShapeDtypeStruct((n, WIDTH), F32),
                   jax.ShapeDtypeStruct((bsz, HEADS, 128, 128), F32),
                   jax.ShapeDtypeStruct((bsz, 8, 128), F32)],
        scratch_shapes=[pltpu.VMEM((HEADS, 128, 128), F32), pltpu.VMEM((8, 128), F32)],
        compiler_params=_cparams("parallel", "arbitrary"),
        name="mlstm_prompt",
    )(u, u, bias_row, norm_row)


def _pool_kernel(xb_ref, w_ref, scale_ref, out_ref, st_ref, xs):
    j = pl.program_id(1)
    tt = xb_ref.shape[0]

    @pl.when(j == 0)
    def _():
        xs[0:16, :] = jnp.zeros((16, WIDTH), F32)

    xb = xb_ref[...]
    xs[16:16 + tt, :] = xb
    lane = lax.broadcasted_iota(jnp.int32, (tt, WIDTH), 1)
    pos = j * tt + lax.broadcasted_iota(jnp.int32, (tt, WIDTH), 0)
    acc = xb
    tot = jnp.zeros((tt, WIDTH), F32)
    cnt = jnp.zeros((tt, WIDTH), F32)
    for k in range(1, 16):
        acc = acc + xs[pl.ds(16 - k, tt), :]
        if k + 1 in POOL_WINDOWS:
            g = POOL_WINDOWS.index(k + 1)
            sel = (lane >= 64 * g) & (lane < 64 * g + 64)
            tot = jnp.where(sel, acc, tot)
            cnt = jnp.where(sel, jnp.minimum(pos + 1, k + 1).astype(F32), cnt)
    pooled = tot / cnt - xb
    out_ref[...] = _dot(pooled.astype(BF16), w_ref[...]) * scale_ref[...]
    xs[0:16, :] = xs[tt:tt + 16, :]

    @pl.when(j == pl.num_programs(1) - 1)
    def _():
        st_ref[0] = xs[1:16, :]


def _pool_prompt(u, bsz, t_len, w_blk, scale_row):
    tt = min(512, t_len)
    nt = t_len // tt
    n = bsz * t_len
    return pl.pallas_call(
        _pool_kernel,
        grid=(bsz, nt),
        in_specs=[pl.BlockSpec((tt, WIDTH), lambda b, j: (b * nt + j, U_XB // WIDTH)),
                  pl.BlockSpec((WIDTH, WIDTH), lambda b, j: (0, 0)),
                  pl.BlockSpec((1, WIDTH), lambda b, j: (0, 0))],
        out_specs=[pl.BlockSpec((tt, WIDTH), lambda b, j: (b * nt + j, 0)),
                   pl.BlockSpec((1, POOL_BUF, WIDTH), lambda b, j: (b, 0, 0))],
        out_shape=[jax.ShapeDtypeStruct((n, WIDTH), F32),
                   jax.ShapeDtypeStruct((bsz, POOL_BUF, WIDTH), F32)],
        scratch_shapes=[pltpu.VMEM((tt + 16, WIDTH), F32)],
        compiler_params=_cparams("parallel", "arbitrary"),
        name="pool_prompt",
    )(u, w_blk, scale_row)


def _ssd_kernel(dxbc_ref, dz_ref, sm_ref, cw_ref, cb_ref, bias_ref, alog_row_ref, alog_col_ref,
                skip_ref, norm_ref, out_ref, cst_ref, hst_ref, xs, hs):
    j = pl.program_id(1)
    L = dxbc_ref.shape[0]

    @pl.when(j == 0)
    def _():
        xs[0:8, :] = jnp.zeros((8, D_CONV_DIM), F32)
        hs[...] = jnp.zeros_like(hs)

    xs[8:8 + L, :] = dxbc_ref[...]
    conv = cb_ref[...] + xs[pl.ds(5, L), :] * cw_ref[0:1, :]
    for tap in range(1, D_CONV):
        conv = conv + xs[pl.ds(5 + tap, L), :] * cw_ref[tap:tap + 1, :]
    conv = _silu(conv)
    xs[0:8, :] = xs[L:L + 8, :]
    x4 = conv[:, 0:256]

    pre = sm_ref[...] + bias_ref[...]
    dtf = _softplus(pre)
    causal, anti = _tri(L)
    cum = _dot_exact(causal.astype(F32), dtf * (-jnp.exp(alog_row_ref[...])))
    dt_t = _softplus(pre.T[SM_DT:SM_DT + 8, :])
    cum_t = _dot_exact(dt_t * (-jnp.exp(alog_col_ref[...])), anti.astype(F32))

    lane = lax.broadcasted_iota(jnp.int32, (L, WIDTH), 1)
    dtb = jnp.zeros((L, WIDTH), F32)
    cumb = jnp.zeros((L, WIDTH), F32)
    for h in range(HEADS):
        sel = (lane >= 64 * h) & (lane < 64 * h + 64)
        dtb = jnp.where(sel, dtf[:, SM_DT + h:SM_DT + h + 1], dtb)
        cumb = jnp.where(sel, cum[:, SM_DT + h:SM_DT + h + 1], cumb)
    xdt = (x4 * dtb).astype(BF16)

    y = jnp.zeros((L, WIDTH), F32)
    y_inter = []
    for g in range(2):
        cg = conv[:, 512 + 128 * g:512 + 128 * g + 128].astype(BF16)
        bg = conv[:, 256 + 128 * g:256 + 128 * g + 128].astype(BF16)
        cb = _dot_nt(cg, bg)
        for hl in range(2):
            h = 2 * g + hl
            ccol = cum[:, SM_DT + h:SM_DT + h + 1]
            crow = cum_t[h:h + 1, :]
            seg = jnp.exp(jnp.where(causal, ccol - crow, NEG_INF))
            yh = _dot((cb * seg).astype(BF16), xdt)
            y = jnp.where((lane >= 64 * h) & (lane < 64 * h + 64), yh, y)
        y_inter.append(_dot_nt(cg, hs[128 * g:128 * g + 128, :].astype(BF16)))
    y = y + jnp.concatenate(y_inter, axis=1) * jnp.exp(cumb) + skip_ref[...] * x4
    y = y * _silu(dz_ref[...])
    for g in range(2):
        yg = y[:, 128 * g:128 * g + 128]
        yg = yg * lax.rsqrt(jnp.mean(yg * yg, -1, keepdims=True) + EPS)
        out_ref[:, 128 * g:128 * g + 128] = yg * norm_ref[:, 128 * g:128 * g + 128]

    x4_t = x4.T
    sub = lax.broadcasted_iota(jnp.int32, (128, L), 0)
    subc = lax.broadcasted_iota(jnp.int32, (128, 1), 0)
    for g in range(2):
        bg = conv[:, 256 + 128 * g:256 + 128 * g + 128].astype(BF16)
        rows, decs = [], []
        for hl in range(2):
            h = 2 * g + hl
            last = cum_t[h:h + 1, L - 1:L]
            rows.append(dt_t[h:h + 1, :] * jnp.exp(last - cum_t[h:h + 1, :]))
            decs.append(jnp.exp(last))
        x2t = x4_t[128 * g:128 * g + 128, :] * jnp.where(sub < 64, rows[0], rows[1])
        dec = jnp.where(subc < 64, decs[0], decs[1])
        hs[128 * g:128 * g + 128, :] = hs[128 * g:128 * g + 128, :] * dec + _dot(x2t.astype(BF16), bg)

    @pl.when(j == pl.num_programs(1) - 1)
    def _():
        cst_ref[0] = xs[5:8, :]
        hst_ref[0] = hs[...]


def _ssd_prompt(u, bsz, t_len, cw, cb, bias_row, alog_row, alog_col, skip_row, norm_row):
    L = min(CHUNK, t_len)
    nc = t_len // L
    n = bsz * t_len
    const = lambda shape: pl.BlockSpec(shape, lambda b, j: (0,) * len(shape))
    return pl.pallas_call(
        _ssd_kernel,
        grid=(bsz, nc),
        in_specs=[pl.BlockSpec((L, D_CONV_DIM), lambda b, j: (b * nc + j, U_DXBC // D_CONV_DIM)),
                  pl.BlockSpec((L, WIDTH), lambda b, j: (b * nc + j, U_DZ // WIDTH)),
                  pl.BlockSpec((L, 128), lambda b, j: (b * nc + j, U_SMALL // 128)),
                  const((D_CONV, D_CONV_DIM)), const((1, D_CONV_DIM)), const((1, 128)), const((1, 128)),
                  const((8, 1)), const((1, WIDTH)), const((1, WIDTH))],
        out_specs=[pl.BlockSpec((L, WIDTH), lambda b, j: (b * nc + j, 0)),
                   pl.BlockSpec((1, D_CONV - 1, D_CONV_DIM), lambda b, j: (b, 0, 0)),
                   pl.BlockSpec((1, HEADS * DH, D_STATE), lambda b, j: (b, 0, 0))],
        out_shape=[jax.ShapeDtypeStruct((n, WIDTH), F32),
                   jax.ShapeDtypeStruct((bsz, D_CONV - 1, D_CONV_DIM), F32),
                   jax.ShapeDtypeStruct((bsz, HEADS * DH, D_STATE), F32)],
        scratch_shapes=[pltpu.VMEM((L + 8, D_CONV_DIM), F32), pltpu.VMEM((HEADS * DH, D_STATE), F32)],
        compiler_params=_cparams("parallel", "arbitrary"),
        name="ssd_prompt",
    )(u, u, u, cw, cb, bias_row, alog_row, alog_col, skip_row, norm_row)


def _mla_prep_kernel(cq_ref, ckv_ref, k1_ref, k2_ref, cs1_ref, cs2_ref, cs1t_ref, cs2t_ref, csa_ref, csb_ref,
                     qn_ref, kvn_ref, wnt_ref, wx1t_ref, wx2t_ref, wukt_ref, hmt_ref,
                     clat_ref, kr_ref, kcat_ref, clatt_ref, qt_ref):
    ql_t = _rms(cq_ref[...], qn_ref[...]).T.astype(BF16)
    qnope_t = _dot(wnt_ref[...], ql_t)
    tail_t = _dot(wx1t_ref[...], ql_t) * cs1t_ref[...] + _dot(wx2t_ref[...], ql_t) * cs2t_ref[...]
    qabs_t = _dot(wukt_ref[...], qnope_t.astype(BF16))
    for h in range(HEADS):
        qt_ref[h, 0:128, :] = qabs_t[128 * h:128 * h + 128, :].astype(BF16)
        qt_ref[h, 128:256, :] = (tail_t * hmt_ref[h]).astype(BF16)
    clat = _rms(ckv_ref[...], kvn_ref[...])
    clat_ref[...] = clat
    clatt_ref[0] = clat.T.astype(BF16)
    k1 = k1_ref[...]
    k2 = k2_ref[...]
    kcat_ref[:, 0:128] = clat.astype(BF16)
    kcat_ref[:, 128:256] = (k1 * cs1_ref[...] + k2 * cs2_ref[...]).astype(BF16)
    kr_ref[...] = k1[:, 0:C_ROPE] * csa_ref[...] + k2[:, 0:C_ROPE] * csb_ref[...]


def _mla_prep(u, t_len, tabs, qn, kvn, wnt, wx1t, wx2t, wukt, hmt):
    n = u.shape[0]
    tm = min(CHUNK, t_len)
    ntab = t_len // tm
    tab = lambda w: pl.BlockSpec((tm, w), lambda i: (i % ntab, 0))
    tabt = pl.BlockSpec((128, tm), lambda i: (0, i % ntab))
    const = lambda shape: pl.BlockSpec(shape, lambda i: (0,) * len(shape))
    col = lambda w, off: pl.BlockSpec((tm, w), lambda i: (i, off // w))
    return pl.pallas_call(
        _mla_prep_kernel,
        grid=(n // tm,),
        in_specs=[col(256, U_CQ), col(128, U_CKV), col(128, U_K1), col(128, U_K2),
                  tab(128), tab(128), tabt, tabt, tab(C_ROPE), tab(C_ROPE),
                  const((1, 256)), const((1, 128)), const((256, 256)), const((128, 256)), const((128, 256)),
                  const((512, 256)), const((HEADS, 128, 1))],
        out_specs=[pl.BlockSpec((tm, 128), lambda i: (i, 0)),
                   pl.BlockSpec((tm, C_ROPE), lambda i: (i, 0)),
                   pl.BlockSpec((tm, 256), lambda i: (i, 0)),
                   pl.BlockSpec((1, 128, tm), lambda i: (i, 0, 0)),
                   pl.BlockSpec((HEADS, 256, tm), lambda i: (i, 0, 0))],
        out_shape=[jax.ShapeDtypeStruct((n, 128), F32),
                   jax.ShapeDtypeStruct((n, C_ROPE), F32),
                   jax.ShapeDtypeStruct((n, 256), BF16),
                   jax.ShapeDtypeStruct((n // tm, 128, tm), BF16),
                   jax.ShapeDtypeStruct((n // tm * HEADS, 256, tm), BF16)],
        compiler_params=_cparams("parallel"),
        name="mla_prep",
    )(u, u, u, u, *tabs, qn, kvn, wnt, wx1t, wx2t, wukt, hmt)


def _flash_kernel(qt_ref, k_ref, ct_ref, wuvt_ref, o_ref):
    qi = pl.program_id(1)
    tq = qt_ref.shape[2]
    key_idx = lax.broadcasted_iota(jnp.int32, (tq, tq), 0)
    qry_idx = lax.broadcasted_iota(jnp.int32, (tq, tq), 1)
    def step(kb, carry, masked):
        k = k_ref[pl.ds(pl.multiple_of(kb * tq, tq), tq), :]
        c_t = ct_ref[kb]
        def scores(h):
            s = _dot(k, qt_ref[h]) * MLA_SCALE
            return jnp.where(key_idx <= qry_idx, s, NEG_INF) if masked else s

        def softmax(h, s):
            m_prev, l_prev, _ = carry[h]
            m_new = jnp.maximum(m_prev, jnp.max(s, axis=0, keepdims=True))
            alpha = jnp.exp(m_prev - m_new)
            p = jnp.exp(s - m_new)
            return m_new, alpha * l_prev + jnp.sum(p, axis=0, keepdims=True), alpha, p.astype(BF16)

        ahead = 2
        out = []
        s_all = [scores(h) for h in range(ahead)]
        for h in range(HEADS):
            if h + ahead < HEADS:
                s_all.append(scores(h + ahead))
            m_new, l_new, alpha, p = softmax(h, s_all[h])
            out.append((m_new, l_new, alpha * carry[h][2] + _dot(c_t, p)))
        return tuple(out)

    init = tuple((jnp.full((1, tq), NEG_INF, F32), jnp.zeros((1, tq), F32), jnp.zeros((128, tq), F32))
                 for _ in range(HEADS))
    carry = lax.fori_loop(0, qi, functools.partial(step, masked=False), init)
    fin = step(qi, carry, True)
    parts = []
    for h in range(HEADS):
        _, l_fin, acc_fin = fin[h]
        o_t = (acc_fin / l_fin).astype(BF16)
        parts.append(_dot(wuvt_ref[64 * h:64 * h + 64, :], o_t))
    o_ref[...] = jnp.concatenate(parts, axis=0).T


def _mla_prompt(q_t, kcat, clat_t, wuv_t, bsz, t_len):
    tq = min(CHUNK, t_len)
    nq = t_len // tq
    n = bsz * t_len
    return pl.pallas_call(
        _flash_kernel,
        grid=(bsz, nq),
        in_specs=[pl.BlockSpec((HEADS, 256, tq), lambda b, i: (b * nq + i, 0, 0)),
                  pl.BlockSpec((t_len, 256), lambda b, i: (b, 0)),
                  pl.BlockSpec((nq, 128, tq), lambda b, i: (b, 0, 0)),
                  pl.BlockSpec((WIDTH, 128), lambda b, i: (0, 0))],
        out_specs=pl.BlockSpec((tq, WIDTH), lambda b, i: (b * nq + i, 0)),
        out_shape=jax.ShapeDtypeStruct((n, WIDTH), F32),
        compiler_params=_cparams("parallel", "arbitrary"),
        name="mla_prompt",
    )(q_t, kcat, clat_t, wuv_t)


def _decode_kernel(pt_ref, q_ref, kn_ref, wuv_ref, ckv_hbm, kr_hbm, o_ref, bufc, bufr, sem, *, layer, n_pages, cp):
    b = pl.program_id(0)
    nb = pl.num_programs(0)
    n_chunks = n_pages // cp

    def copies(bb, c, slot, p):
        page = pt_ref[bb * n_pages + c * cp + p]
        return (pltpu.make_async_copy(ckv_hbm.at[layer, page], bufc.at[slot, p], sem.at[0, slot]),
                pltpu.make_async_copy(kr_hbm.at[layer, page], bufr.at[slot, p], sem.at[1, slot]))

    def start_chunk(bb, c, slot):
        for p in range(cp):
            for cpy in copies(bb, c, slot, p):
                cpy.start()

    def wait_chunk(bb, c, slot):
        for p in range(cp):
            for cpy in copies(bb, c, slot, p):
                cpy.wait()

    @pl.when(b == 0)
    def _():
        start_chunk(0, 0, 0)

    q = q_ref[0]
    kn = kn_ref[0].astype(BF16).astype(F32)
    s_new = jnp.sum(q.astype(F32) * kn, -1, keepdims=True) * MLA_SCALE
    m0 = s_new
    l0 = jnp.ones((8, 1), F32)
    acc0 = jnp.broadcast_to(kn[:, 0:128], (8, 128))

    def body(c, carry):
        m_prev, l_prev, acc = carry
        slot = (b * n_chunks + c) % 2

        @pl.when(c + 1 < n_chunks)
        def _():
            start_chunk(b, c + 1, 1 - slot)

        @pl.when((c + 1 == n_chunks) & (b + 1 < nb))
        def _():
            start_chunk(b + 1, 0, 1 - slot)

        wait_chunk(b, c, slot)
        kc = bufc[slot].reshape(cp * 128, 128).astype(BF16)
        kr = bufr[slot].reshape(cp * 128, C_ROPE).astype(BF16)
        kfull = jnp.concatenate([kc, kr, jnp.zeros((cp * 128, 128 - C_ROPE), BF16)], axis=1)
        s = _dot_nt(q, kfull) * MLA_SCALE
        m_new = jnp.maximum(m_prev, jnp.max(s, -1, keepdims=True))
        alpha = jnp.exp(m_prev - m_new)
        p = jnp.exp(s - m_new)
        l_new = alpha * l_prev + jnp.sum(p, -1, keepdims=True)
        acc_new = alpha * acc + _dot(p.astype(BF16), kc)
        return m_new, l_new, acc_new

    _, l_fin, acc_fin = lax.fori_loop(0, n_chunks, body, (m0, l0, acc0))
    o = (acc_fin / l_fin).astype(BF16)
    of = _dot(o, wuv_ref[...])
    row = lax.broadcasted_iota(jnp.int32, (8, WIDTH), 0)
    lane = lax.broadcasted_iota(jnp.int32, (8, WIDTH), 1)
    diag = (lane >= 64 * row) & (lane < 64 * row + 64)
    o_ref[0] = jnp.sum(jnp.where(diag, of, 0.0), axis=0, keepdims=True)


def _mla_decode(page_table, q8, k_new, wuv, cache_ckv, cache_kr, layer):
    bsz, n_pages = page_table.shape
    cp = min(PAGES_PER_STEP, n_pages // 2)
    page = cache_ckv.shape[2]
    kern = functools.partial(_decode_kernel, layer=layer, n_pages=n_pages, cp=cp)
    grid_spec = pltpu.PrefetchScalarGridSpec(
        num_scalar_prefetch=1,
        grid=(bsz,),
        in_specs=[pl.BlockSpec((1, 8, 256), lambda b, pt: (b, 0, 0)),
                  pl.BlockSpec((1, 1, 256), lambda b, pt: (b, 0, 0)),
                  pl.BlockSpec((128, WIDTH), lambda b, pt: (0, 0)),
                  pl.BlockSpec(memory_space=pl.ANY),
                  pl.BlockSpec(memory_space=pl.ANY)],
        out_specs=pl.BlockSpec((1, 1, WIDTH), lambda b, pt: (b, 0, 0)),
        scratch_shapes=[pltpu.VMEM((2, cp, page, 128), F32), pltpu.VMEM((2, cp, page, C_ROPE), F32),
                        pltpu.SemaphoreType.DMA((2, 2))],
    )
    return pl.pallas_call(
        kern,
        grid_spec=grid_spec,
        out_shape=jax.ShapeDtypeStruct((bsz, 1, WIDTH), F32),
        compiler_params=_cparams("arbitrary"),
        name="mla_decode",
    )(page_table.reshape(-1), q8, k_new, wuv, cache_ckv, cache_kr)


def _mlstm_step_kernel(qr_ref, kr_ref, vr_ref, or_ref, qc_ref, kc_ref, ig_ref, fg_ref, c_ref, n_ref, m_ref,
                       bi_ref, bf_ref, norm_ref, out_ref, c1_ref, n1_ref, m1_ref):
    q = qr_ref[...]
    k = kr_ref[...] * 0.125
    ig = ig_ref[...] + bi_ref[...]
    lf = _log_sigmoid(fg_ref[...] + bf_ref[...])
    m0 = m_ref[...]
    c0 = c_ref[...]
    n0 = n_ref[...]
    inter = lf + m0
    mt = jnp.maximum(ig, inter)
    s = jnp.exp(ig - mt) * jnp.sum(q * k, -1, keepdims=True)
    g = jnp.exp(inter - mt)
    qc = jnp.sum(qc_ref[...] * c0, axis=2, keepdims=True)
    num = s * vr_ref[...] + g * qc
    den = s + g * jnp.sum(q * n0, -1, keepdims=True)
    hh = num / jnp.maximum(jnp.abs(den), jnp.exp(-mt))
    mu = jnp.mean(hh, -1, keepdims=True)
    dv = hh - mu
    var = jnp.mean(dv * dv, -1, keepdims=True)
    hn = dv * lax.rsqrt(var + EPS) * norm_ref[...]
    out_ref[...] = jax.nn.sigmoid(or_ref[...]) * hn
    w = jnp.exp(ig - mt)
    decay = jnp.exp(inter - mt)
    c1_ref[...] = decay * c0 + (w * 0.125 * kc_ref[...]) * vr_ref[...]
    n1_ref[...] = decay * n0 + w * k
    m1_ref[...] = mt


def _mlstm_step(q, k, v, o, ig, fg, c0, n0, m0, bi, bf, norm):
    bsz = q.shape[0]
    bb = min(16, bsz)
    r4 = lambda a: a.reshape(bsz, HEADS, 1, DH)
    c4 = lambda a: a.reshape(bsz, HEADS, DH, 1)
    s4 = lambda a: a.reshape(bsz, HEADS, 1, 1)
    blk = lambda d2, d3: pl.BlockSpec((bb, HEADS, d2, d3), lambda i: (i, 0, 0, 0))
    par = lambda d2, d3: pl.BlockSpec((1, HEADS, d2, d3), lambda i: (0, 0, 0, 0))
    return pl.pallas_call(
        _mlstm_step_kernel,
        grid=(bsz // bb,),
        in_specs=[blk(1, DH)] * 4 + [blk(DH, 1)] * 2 + [blk(1, 1)] * 2
                 + [blk(DH, DH), blk(1, DH), blk(1, 1), par(1, 1), par(1, 1), par(1, DH)],
        out_specs=[blk(1, DH), blk(DH, DH), blk(1, DH), blk(1, 1)],
        out_shape=[jax.ShapeDtypeStruct((bsz, HEADS, 1, DH), F32),
                   jax.ShapeDtypeStruct((bsz, HEADS, DH, DH), F32),
                   jax.ShapeDtypeStruct((bsz, HEADS, 1, DH), F32),
                   jax.ShapeDtypeStruct((bsz, HEADS, 1, 1), F32)],
        compiler_params=_cparams("parallel"),
        name="mlstm_step",
    )(r4(q), r4(k), r4(v), r4(o), c4(q), c4(k), s4(ig), s4(fg), c0, r4(n0), s4(m0),
      bi.reshape(1, HEADS, 1, 1), bf.reshape(1, HEADS, 1, 1), norm.reshape(1, HEADS, 1, DH))


def _pool_step_kernel(xb_ref, st_ref, pooled_ref, st1_ref, *, pos):
    xb = xb_ref[...]
    lane = lax.broadcasted_iota(jnp.int32, xb.shape, 2)
    acc = xb
    pooled = jnp.zeros_like(xb)
    for k in range(1, 16):
        acc = acc + st_ref[:, POOL_BUF - k:POOL_BUF - k + 1, :]
        if k + 1 in POOL_WINDOWS:
            g = POOL_WINDOWS.index(k + 1)
            cnt = float(min(pos + 1, k + 1))
            pooled = jnp.where((lane >= 64 * g) & (lane < 64 * g + 64), acc / cnt - xb, pooled)
    pooled_ref[...] = pooled
    st1_ref[:, 0:POOL_BUF - 1, :] = st_ref[:, 1:POOL_BUF, :]
    st1_ref[:, POOL_BUF - 1:POOL_BUF, :] = xb


def _pool_step(xb, state, pos):
    bsz = xb.shape[0]
    bb = min(32, bsz)
    return pl.pallas_call(
        functools.partial(_pool_step_kernel, pos=pos),
        grid=(bsz // bb,),
        in_specs=[pl.BlockSpec((bb, 1, WIDTH), lambda i: (i, 0, 0)),
                  pl.BlockSpec((bb, POOL_BUF, WIDTH), lambda i: (i, 0, 0))],
        out_specs=[pl.BlockSpec((bb, 1, WIDTH), lambda i: (i, 0, 0)),
                   pl.BlockSpec((bb, POOL_BUF, WIDTH), lambda i: (i, 0, 0))],
        out_shape=[jax.ShapeDtypeStruct((bsz, 1, WIDTH), F32),
                   jax.ShapeDtypeStruct((bsz, POOL_BUF, WIDTH), F32)],
        compiler_params=_cparams("parallel"),
        name="pool_step",
    )(xb.reshape(bsz, 1, WIDTH), state)


def _pool_lin_kernel(p_ref, w_ref, scale_ref, o_ref):
    o_ref[...] = _dot(p_ref[...].astype(BF16), w_ref[...]) * scale_ref[...]


def _pool_lin(pooled, w_blk, scale_row):
    n = pooled.shape[0]
    return pl.pallas_call(
        _pool_lin_kernel,
        out_shape=jax.ShapeDtypeStruct((n, WIDTH), F32),
        name="pool_lin",
    )(pooled, w_blk, scale_row)


def _ssd_step_kernel(x0_ref, x1_ref, x2_ref, x3_ref, bc_ref, st_ref, full_ref, dz_ref, ddt_ref, h_ref,
                     cwx_ref, cbx_ref, cwbc_ref, cbbc_ref, dtb_ref, alog_ref, skip_ref, norm_ref,
                     out_ref, st1_ref, h1_ref):
    xc = cbx_ref[...] + x0_ref[...] * cwx_ref[0] + x1_ref[...] * cwx_ref[1] \
        + x2_ref[...] * cwx_ref[2] + x3_ref[...] * cwx_ref[3]
    xc = _silu(xc)
    bc = cbbc_ref[...] + bc_ref[...] * cwbc_ref[3:4, :]
    for tap in range(D_CONV - 1):
        bc = bc + st_ref[:, tap:tap + 1, 256:768] * cwbc_ref[tap:tap + 1, :]
    bc = _silu(bc)
    dt = _softplus(ddt_ref[...] + dtb_ref[...])
    dec = jnp.exp(dt * (-jnp.exp(alog_ref[...])))
    xdt = xc * dt
    ys = []
    for g in range(2):
        bg = bc[:, :, 128 * g:128 * g + 128][:, None]
        cg = bc[:, :, 256 + 128 * g:256 + 128 * g + 128][:, None]
        h1 = h_ref[:, 2 * g:2 * g + 2] * dec[:, 2 * g:2 * g + 2] + xdt[:, 2 * g:2 * g + 2] * bg
        h1_ref[:, 2 * g:2 * g + 2] = h1
        ys.append(jnp.sum(h1 * cg, -1, keepdims=True))
    for g in range(2):
        y = ys[g] + skip_ref[:, 2 * g:2 * g + 2] * xc[:, 2 * g:2 * g + 2]
        y = y * _silu(dz_ref[:, 2 * g:2 * g + 2])
        ss = jnp.sum(y * y, axis=2, keepdims=True)
        ms = (ss[:, 0:1] + ss[:, 1:2]) * (1.0 / 128)
        out_ref[:, 2 * g:2 * g + 2] = y * lax.rsqrt(ms + EPS) * norm_ref[:, 2 * g:2 * g + 2]
    st1_ref[:, 0:D_CONV - 2, :] = st_ref[:, 1:D_CONV - 1, :]
    st1_ref[:, D_CONV - 2:D_CONV - 1, :] = full_ref[...]


def _ssd_step(dxbc, dz, ddt, state_conv, h0, cw, cb, dt_bias, a_log, skip, norm):
    bsz = dxbc.shape[0]
    bb = min(8, bsz)
    c4 = lambda a: a.reshape(bsz, HEADS, DH, 1)
    b4 = lambda d1, d2, d3: pl.BlockSpec((bb, d1, d2, d3), lambda i: (i, 0, 0, 0))
    b3 = lambda d1, d2: pl.BlockSpec((bb, d1, d2), lambda i: (i, 0, 0))
    par = lambda shape: pl.BlockSpec(shape, lambda i: (0,) * len(shape))
    col = b4(HEADS, DH, 1)
    return pl.pallas_call(
        _ssd_step_kernel,
        grid=(bsz // bb,),
        in_specs=[col, col, col, col, b3(1, 512), b3(D_CONV - 1, D_CONV_DIM), b3(1, D_CONV_DIM), col,
                  b4(HEADS, 1, 1), b4(HEADS, DH, D_STATE),
                  par((D_CONV, HEADS, DH, 1)), par((1, HEADS, DH, 1)), par((D_CONV, 512)), par((1, 512)),
                  par((1, HEADS, 1, 1)), par((1, HEADS, 1, 1)), par((1, HEADS, 1, 1)), par((1, HEADS, DH, 1))],
        out_specs=[col, b3(D_CONV - 1, D_CONV_DIM), b4(HEADS, DH, D_STATE)],
        out_shape=[jax.ShapeDtypeStruct((bsz, HEADS, DH, 1), F32),
                   jax.ShapeDtypeStruct((bsz, D_CONV - 1, D_CONV_DIM), F32),
                   jax.ShapeDtypeStruct((bsz, HEADS, DH, D_STATE), F32)],
        compiler_params=_cparams("parallel"),
        name="ssd_step",
    )(c4(state_conv[:, 0, :256]), c4(state_conv[:, 1, :256]), c4(state_conv[:, 2, :256]), c4(dxbc[:, :256]),
      dxbc[:, 256:].reshape(bsz, 1, 512), state_conv, dxbc.reshape(bsz, 1, D_CONV_DIM), c4(dz),
      ddt.reshape(bsz, HEADS, 1, 1), h0,
      cw[:, :256].reshape(D_CONV, HEADS, DH, 1), cb[:256].reshape(1, HEADS, DH, 1), cw[:, 256:],
      cb[256:].reshape(1, 512), dt_bias.reshape(1, HEADS, 1, 1), a_log.reshape(1, HEADS, 1, 1),
      skip.reshape(1, HEADS, 1, 1), norm.reshape(1, HEADS, DH, 1))


def _mla_prep_s_kernel(cq_ref, ckv_ref, k1_ref, k2_ref, csa_ref, csb_ref, qn_ref, kvn_ref, wn_ref, wxa_ref,
                       wxb_ref, wuk_ref, clat_ref, kr_ref, qa_ref, qr_ref):
    ql = _rms(cq_ref[...], qn_ref[...]).astype(BF16)
    qabs = _dot(_dot(ql, wn_ref[...]).astype(BF16), wuk_ref[...])
    csa = csa_ref[...]
    csb = csb_ref[...]
    for h in range(HEADS):
        qa_ref[h] = qabs[:, 128 * h:128 * h + 128].astype(BF16)
        qr_ref[h] = (_dot(ql, wxa_ref[h]) * csa + _dot(ql, wxb_ref[h]) * csb).astype(BF16)
    clat_ref[...] = _rms(ckv_ref[...], kvn_ref[...])
    kr_ref[...] = k1_ref[:, 0:C_ROPE] * csa + k2_ref[:, 0:C_ROPE] * csb


def _mla_prep_sample(u, csa, csb, qn, kvn, wn, wxa, wxb, wuk):
    n = u.shape[0]
    const = lambda shape: pl.BlockSpec(shape, lambda i: (0,) * len(shape))
    col = lambda w, off: pl.BlockSpec((n, w), lambda i: (0, off // w))
    return pl.pallas_call(
        _mla_prep_s_kernel,
        grid=(1,),
        in_specs=[col(256, U_CQ), col(128, U_CKV), col(128, U_K1), col(128, U_K2),
                  const((1, C_ROPE)), const((1, C_ROPE)), const((1, 256)), const((1, 128)), const((256, 256)),
                  const((HEADS, 256, C_ROPE)), const((HEADS, 256, C_ROPE)), const((256, 512))],
        out_specs=[const((n, 128)), const((n, C_ROPE)), const((HEADS, n, 128)), const((HEADS, n, C_ROPE))],
        out_shape=[jax.ShapeDtypeStruct((n, 128), F32),
                   jax.ShapeDtypeStruct((n, C_ROPE), F32),
                   jax.ShapeDtypeStruct((HEADS, n, 128), BF16),
                   jax.ShapeDtypeStruct((HEADS, n, C_ROPE), BF16)],
        compiler_params=_cparams("arbitrary"),
        name="mla_prep_sample",
    )(u, u, u, u, csa, csb, qn, kvn, wn, wxa, wxb, wuk)


def _merge_kernel(x_ref, oa_ref, ob_ref, oc_ref, od_ref, g0_ref, g1_ref, g2_ref, g3_ref, wb_ref, wo_ref,
                  gn_ref, out_ref):
    mixed = None
    for i, (o_ref, g_ref) in enumerate(((oa_ref, g0_ref), (ob_ref, g1_ref), (oc_ref, g2_ref), (od_ref, g3_ref))):
        term = jax.nn.sigmoid(g_ref[...]) * _dot(o_ref[...].astype(BF16), wb_ref[i])
        mixed = term if mixed is None else mixed + term
    y = _dot(mixed.astype(BF16), wo_ref[...])
    out_ref[...] = x_ref[...] + _rms(y, gn_ref[...])


def _merge(x, outs, u, wb, wo, gn):
    n, d = x.shape
    tm = min(512, n)
    row = lambda w: pl.BlockSpec((tm, w), lambda i: (i, 0))
    gate = lambda k: pl.BlockSpec((tm, d), lambda i: (i, U_GATES // d + k))
    return pl.pallas_call(
        _merge_kernel,
        grid=(n // tm,),
        in_specs=[row(d)] + [row(WIDTH)] * 4 + [gate(k) for k in range(4)]
                 + [pl.BlockSpec((4, WIDTH, d), lambda i: (0, 0, 0)), pl.BlockSpec((d, d), lambda i: (0, 0)),
                    pl.BlockSpec((1, d), lambda i: (0, 0))],
        out_specs=row(d),
        out_shape=jax.ShapeDtypeStruct((n, d), F32),
        compiler_params=_cparams("parallel"),
        name="merge",
    )(x, *outs, u, u, u, u, wb, wo, gn)


def _ffn_kernel(x_ref, g1_ref, wg_ref, wu_ref, wd_ref, g2_ref, out_ref, xn_scr, acc_scr):
    k = pl.program_id(1)

    @pl.when(k == 0)
    def _():
        xn_scr[...] = _rms(x_ref[...], g1_ref[...]).astype(BF16)
        acc_scr[...] = jnp.zeros_like(acc_scr)

    xn = xn_scr[...]
    hact = _silu(_dot(xn, wg_ref[...])) * _dot(xn, wu_ref[...])
    acc_scr[...] += _dot(hact.astype(BF16), wd_ref[...])

    @pl.when(k == pl.num_programs(1) - 1)
    def _():
        out_ref[...] = x_ref[...] + _rms(acc_scr[...], g2_ref[...])


def _ffn(x, g1, wgu, wd, g2):
    n, d = x.shape
    hid = wd.shape[0]
    tm = min(512, n)
    nk = 2
    th = hid // nk
    return pl.pallas_call(
        _ffn_kernel,
        grid=(n // tm, nk),
        in_specs=[pl.BlockSpec((tm, d), lambda i, k: (i, 0)),
                  pl.BlockSpec((1, d), lambda i, k: (0, 0)),
                  pl.BlockSpec((d, th), lambda i, k: (0, k)),
                  pl.BlockSpec((d, th), lambda i, k: (0, k + nk)),
                  pl.BlockSpec((th, d), lambda i, k: (k, 0)),
                  pl.BlockSpec((1, d), lambda i, k: (0, 0))],
        out_specs=pl.BlockSpec((tm, d), lambda i, k: (i, 0)),
        out_shape=jax.ShapeDtypeStruct((n, d), F32),
        scratch_shapes=[pltpu.VMEM((tm, d), BF16), pltpu.VMEM((tm, d), F32)],
        compiler_params=_cparams("parallel", "arbitrary"),
        name="ffn",
    )(x, g1, wgu, wgu, wd, g2)


def _lane_row(width, pieces):
    row = jnp.zeros((1, width), F32)
    for off, vals in pieces:
        row = row.at[0, off:off + vals.shape[0]].set(vals.astype(F32))
    return row


def _layer_params(l, n_mix_pre, n_mix_post, n_ffn_pre, n_ffn_post, w_in, a_bi, a_bf, a_norm, b_w, b_scale,
                  c_qnorm, c_wuq, c_kvnorm, c_wuk, c_wuv, d_conv_w, d_conv_b, d_dt_bias, d_a_log, d_skip, d_norm,
                  w_branch, w_o, w_gu, w_down):
    offs = np.concatenate([[0], np.cumsum(IN_SPLITS)]).tolist()
    seg = [w_in[l][:, offs[i]:offs[i + 1]] for i in range(len(IN_SPLITS))]
    aq, ak, av, ao, ai, af, xb, cq, ckv, ckr, dz, dxbc, ddt, gates = seg
    z = lambda w: jnp.zeros((D_MODEL, w), F32)
    small = jnp.concatenate([ckr, ai, z(4), af, z(4), ddt, z(128 - SM_DT - 4)], 1)
    k1 = jnp.tile(ckr[:, :16], (1, 8))
    k2 = jnp.tile(ckr[:, 16:], (1, 8))
    p = {}
    p["w_in"] = jnp.concatenate([aq, ak, av, ao, xb, cq, dxbc, dz, ckv, small, k1, k2, gates], 1).astype(BF16)
    p["n_mix_pre"] = n_mix_pre[l][None]
    p["n_mix_post"] = n_mix_post[l][None]
    p["n_ffn_pre"] = n_ffn_pre[l][None]
    p["n_ffn_post"] = n_ffn_post[l][None]
    p["a_bias_row"] = _lane_row(128, [(SM_IG, a_bi[l]), (SM_FG, a_bf[l])])
    p["a_norm_row"] = a_norm[l][None]
    p["a_bi"], p["a_bf"], p["a_norm"] = a_bi[l], a_bf[l], a_norm[l]
    wblk = jnp.zeros((WIDTH, WIDTH), F32)
    for g in range(4):
        wblk = wblk.at[64 * g:64 * g + 64, 64 * g:64 * g + 64].set(b_w[l][g])
    p["b_wblk"] = wblk.astype(BF16)
    p["b_scale_row"] = b_scale[l][None]
    wuq = c_wuq[l].reshape(256, HEADS, 96)
    p["c_wn"] = wuq[:, :, :64].reshape(256, 256).astype(BF16)
    p["c_wn_t"] = p["c_wn"].T
    wx1 = wuq[:, :, 64:80]
    wx2 = wuq[:, :, 80:96]
    p["c_wx1_t"] = jnp.tile(wx1.reshape(256, 64), (1, 2)).astype(BF16).T
    p["c_wx2_t"] = jnp.tile(wx2.reshape(256, 64), (1, 2)).astype(BF16).T
    p["c_wxa"] = jnp.concatenate([wx1, wx1], -1).transpose(1, 0, 2).astype(BF16)
    p["c_wxb"] = jnp.concatenate([wx2, wx2], -1).transpose(1, 0, 2).astype(BF16)
    wuk_bd = jnp.zeros((256, 512), F32)
    for h in range(HEADS):
        wuk_bd = wuk_bd.at[64 * h:64 * h + 64, 128 * h:128 * h + 128].set(c_wuk[l][:, h, :].T)
    p["c_wuk_bd"] = wuk_bd.astype(BF16)
    p["c_wuk_bd_t"] = p["c_wuk_bd"].T
    p["c_wuv"] = c_wuv[l].reshape(C_KV_RANK, WIDTH).astype(BF16)
    p["c_wuv_t"] = p["c_wuv"].T
    p["c_qnorm"] = c_qnorm[l][None]
    p["c_kvnorm"] = c_kvnorm[l][None]
    p["d_conv_w"] = d_conv_w[l]
    p["d_conv_b"] = d_conv_b[l]
    p["d_conv_b_row"] = d_conv_b[l][None]
    p["d_bias_row"] = _lane_row(128, [(SM_DT, d_dt_bias[l])])
    p["d_alog_row"] = _lane_row(128, [(SM_DT, d_a_log[l])])
    p["d_alog_col"] = _lane_row(8, [(0, d_a_log[l])]).reshape(8, 1)
    p["d_skip_row"] = jnp.repeat(d_skip[l], DH)[None]
    p["d_norm_row"] = d_norm[l][None]
    p["d_dt_bias"], p["d_a_log"], p["d_skip"], p["d_norm"] = d_dt_bias[l], d_a_log[l], d_skip[l], d_norm[l]
    p["w_branch"] = w_branch[l].astype(BF16)
    p["w_o"] = w_o[l].astype(BF16)
    p["w_gu"] = w_gu[l].astype(BF16)
    p["w_down"] = w_down[l].astype(BF16)
    return p


def _rope_tables(pos):
    inv = ROPE_THETA ** (-jnp.arange(16, dtype=F32) / 16)
    ang = pos.astype(F32)[:, None] * inv[None, :]
    cos, sin = jnp.cos(ang), jnp.sin(ang)
    cos64, sin64 = jnp.tile(cos, (1, 4)), jnp.tile(sin, (1, 4))
    cs1 = jnp.concatenate([cos64, sin64], 1)
    cs2 = jnp.concatenate([-sin64, cos64], 1)
    csa = jnp.concatenate([cos, sin], 1)
    csb = jnp.concatenate([-sin, cos], 1)
    return cs1, cs2, cs1.T, cs2.T, csa, csb


def _head_lane_mask():
    lane = np.arange(128)
    return jnp.asarray(((lane % 64) // 16)[None, :] == np.arange(HEADS)[:, None], F32)[:, :, None]


def _prompt_layer(x, bsz, t_len, p, tabs, hm):
    u = _in_proj(x, p["n_mix_pre"], p["w_in"])
    out_a, c_pad, m_pad = _mlstm_prompt(u, bsz, t_len, p["a_bias_row"], p["a_norm_row"])
    out_b, pool1 = _pool_prompt(u, bsz, t_len, p["b_wblk"], p["b_scale_row"])
    out_d, conv1, ssm1 = _ssd_prompt(u, bsz, t_len, p["d_conv_w"], p["d_conv_b_row"], p["d_bias_row"],
                                     p["d_alog_row"], p["d_alog_col"], p["d_skip_row"], p["d_norm_row"])
    c_lat, k_rope, kcat, clat_t, q_t = _mla_prep(u, t_len, tabs, p["c_qnorm"], p["c_kvnorm"], p["c_wn_t"],
                                                 p["c_wx1_t"], p["c_wx2_t"], p["c_wuk_bd_t"], hm)
    out_c = _mla_prompt(q_t, kcat, clat_t, p["c_wuv_t"], bsz, t_len)
    x = _merge(x, (out_a, out_b, out_c, out_d), u, p["w_branch"], p["w_o"], p["n_mix_post"])
    x = _ffn(x, p["n_ffn_pre"], p["w_gu"], p["w_down"], p["n_ffn_post"])
    c1 = jnp.stack([c_pad[:, h, 64 * (h % 2):64 * (h % 2) + 64, :64] for h in range(HEADS)], 1)
    n1 = jnp.stack([c_pad[:, h, 64 * (h % 2):64 * (h % 2) + 64, 64] for h in range(HEADS)], 1)
    m1 = m_pad[:, :HEADS, 0]
    state = (c1, n1, m1, pool1, conv1, ssm1.reshape(bsz, HEADS, DH, D_STATE))
    rows = (c_lat.reshape(bsz, t_len, C_KV_RANK), k_rope.reshape(bsz, t_len, C_ROPE))
    return x, state, rows


def _sample_layer(x, p, layer, tabs_s, state, page_table, cache_ckv, cache_kr, past_len):
    c0, n0, m0, pool0, conv0, ssm0 = state
    bsz = x.shape[0]
    u = _in_proj(x, p["n_mix_pre"], p["w_in"])
    sm = u[:, U_SMALL:U_SMALL + 128]
    out_a, c1, n1, m1 = _mlstm_step(u[:, 0:256], u[:, 256:512], u[:, 512:768], u[:, 768:1024],
                                    sm[:, SM_IG:SM_IG + 4], sm[:, SM_FG:SM_FG + 4], c0, n0, m0,
                                    p["a_bi"], p["a_bf"], p["a_norm"])
    pooled, pool1 = _pool_step(u[:, U_XB:U_XB + WIDTH], pool0, past_len)
    out_b = _pool_lin(pooled.reshape(bsz, WIDTH), p["b_wblk"], p["b_scale_row"])
    out_d, conv1, ssm1 = _ssd_step(u[:, U_DXBC:U_DXBC + D_CONV_DIM], u[:, U_DZ:U_DZ + WIDTH],
                                   sm[:, SM_DT:SM_DT + 4], conv0, ssm0, p["d_conv_w"], p["d_conv_b"],
                                   p["d_dt_bias"], p["d_a_log"], p["d_skip"], p["d_norm"])
    c_lat, k_rope, qa, qr = _mla_prep_sample(u, tabs_s[4], tabs_s[5], p["c_qnorm"], p["c_kvnorm"], p["c_wn"],
                                             p["c_wxa"], p["c_wxb"], p["c_wuk_bd"])
    lane_pad = 256 - C_KV_RANK - C_ROPE
    q8 = jnp.pad(jnp.concatenate([qa, qr], -1).transpose(1, 0, 2), ((0, 0), (0, 8 - HEADS), (0, lane_pad)))
    k_new = jnp.pad(jnp.concatenate([c_lat, k_rope], -1), ((0, 0), (0, lane_pad))).reshape(bsz, 1, 256)
    out_c = _mla_decode(page_table, q8, k_new, p["c_wuv"], cache_ckv, cache_kr, layer)
    outs = (out_a.reshape(bsz, WIDTH), out_b, out_c.reshape(bsz, WIDTH), out_d.reshape(bsz, WIDTH))
    x = _merge(x, outs, u, p["w_branch"], p["w_o"], p["n_mix_post"])
    x = _ffn(x, p["n_ffn_pre"], p["w_gu"], p["w_down"], p["n_ffn_post"])
    new_state = (c1, n1.reshape(bsz, HEADS, DH), m1.reshape(bsz, HEADS), pool1, conv1, ssm1)
    rows = (c_lat.reshape(bsz, 1, C_KV_RANK), k_rope.reshape(bsz, 1, C_ROPE))
    return x, new_state, rows


def kernel(x_prompt, x_sample, state_mlstm_c, state_mlstm_n, state_mlstm_m, state_pool, cache_ckv, cache_krope,
           page_table, state_conv, state_ssm, n_mix_pre, n_mix_post, n_ffn_pre, n_ffn_post, w_in, a_bi, a_bf,
           a_norm, b_w, b_scale, c_qnorm, c_wuq, c_kvnorm, c_wuk, c_wuv, d_conv_w, d_conv_b, d_dt_bias, d_a_log,
           d_skip, d_norm, w_branch, w_o, w_gu, w_down):
    bp, t_len, d = x_prompt.shape
    bs = x_sample.shape[0]
    assert x_sample.shape[1] == 1 and d == D_MODEL
    depth = w_in.shape[0]
    past_len = page_table.shape[1] * cache_ckv.shape[2]
    tabs_p = _rope_tables(jnp.arange(t_len))
    tabs_s = _rope_tables(jnp.full((1,), past_len))
    hm = _head_lane_mask()
    yp = x_prompt.reshape(bp * t_len, d)
    ys = x_sample.reshape(bs, d)
    st_p, st_s, rows_p, rows_s = [], [], [], []
    for l in range(depth):
        p = _layer_params(l, n_mix_pre, n_mix_post, n_ffn_pre, n_ffn_post, w_in, a_bi, a_bf, a_norm, b_w, b_scale,
                          c_qnorm, c_wuq, c_kvnorm, c_wuk, c_wuv, d_conv_w, d_conv_b, d_dt_bias, d_a_log, d_skip,
                          d_norm, w_branch, w_o, w_gu, w_down)
        yp, sp, rp = _prompt_layer(yp, bp, t_len, p, tabs_p, hm)
        in_state = (state_mlstm_c[l], state_mlstm_n[l], state_mlstm_m[l], state_pool[l], state_conv[l], state_ssm[l])
        ys, ss, rs = _sample_layer(ys, p, l, tabs_s, in_state, page_table, cache_ckv, cache_krope, past_len)
        st_p.append(sp)
        st_s.append(ss)
        rows_p.append(rp)
        rows_s.append(rs)

    stk = lambda lst, i: jnp.stack([s[i] for s in lst], 0)
    return (yp.reshape(bp, t_len, d), ys.reshape(bs, 1, d),
            stk(rows_p, 0), stk(rows_p, 1), stk(rows_s, 0), stk(rows_s, 1),
            stk(st_p, 0), stk(st_p, 1), stk(st_p, 2), stk(st_s, 0), stk(st_s, 1), stk(st_s, 2),
            stk(st_p, 3), stk(st_s, 3), stk(st_p, 4), stk(st_s, 4), stk(st_p, 5), stk(st_s, 5))
```

```python
import functools
import math

import jax
import jax.numpy as jnp
import numpy as np
from jax import lax
from jax.experimental import pallas as pl
from jax.experimental.pallas import tpu as pltpu

F32 = jnp.float32
BF16 = jnp.bfloat16
HIGHEST = lax.Precision.HIGHEST
NEG_INF = float("-inf")

EPS = 1e-6
D_MODEL = 1024
HEADS = 4
DH = 64
WIDTH = 256
C_ROPE = 32
C_KV_RANK = 128
ROPE_THETA = 10000.0
MLA_SCALE = (64 + C_ROPE) ** -0.5
D_STATE = 128
D_CONV = 4
D_CONV_DIM = 768
POOL_WINDOWS = (2, 4, 8, 16)
POOL_BUF = 15
FFN_HIDDEN = 2816
IN_SPLITS = (256, 256, 256, 256, 4, 4, 256, 256, 128, 32, 256, 768, 4, 4096)

U_MLSTM = 0
U_XB = 1024
U_CQ = 1280
U_DXBC = 1536
U_DZ = 2304
U_CKV = 2560
U_SMALL = 2688
U_K1 = 2816
U_K2 = 2944
U_GATES = 3072
U_WIDTH = 7168
SM_IG, SM_FG, SM_DT = 32, 40, 48

VMEM_LIMIT = 56 * 1024 * 1024
CHUNK = 256
PAGES_PER_STEP = 16
DECODE_SLOTS = 4


def _cparams(*sem):
    return pltpu.CompilerParams(dimension_semantics=sem, vmem_limit_bytes=VMEM_LIMIT)


def _rms(xf, g):
    return xf * lax.rsqrt(jnp.mean(xf * xf, -1, keepdims=True) + EPS) * g


def _softplus(x):
    return jnp.maximum(x, 0.0) + jnp.log1p(jnp.exp(-jnp.abs(x)))


def _log_sigmoid(x):
    return -_softplus(-x)


def _silu(x):
    return x * jax.nn.sigmoid(x)


def _dot(a, b):
    return jnp.dot(a, b, preferred_element_type=F32)


def _dot_nt(a, b):
    return lax.dot_general(a, b, (((1,), (1,)), ((), ())), preferred_element_type=F32)


def _dot_exact(a, b):
    return jnp.dot(a, b, precision=HIGHEST, preferred_element_type=F32)


def _tri(n):
    row = lax.broadcasted_iota(jnp.int32, (n, n), 0)
    col = lax.broadcasted_iota(jnp.int32, (n, n), 1)
    return row >= col, row <= col


def _in_proj_kernel(x_ref, g_ref, w_ref, o_ref, xn_ref):
    @pl.when(pl.program_id(1) == 0)
    def _():
        xn_ref[...] = _rms(x_ref[...], g_ref[...]).astype(BF16)

    o_ref[...] = _dot(xn_ref[...], w_ref[...])


def _in_proj(x, g, w):
    n, d = x.shape
    wn = w.shape[1]
    tm = min(n, 1024)
    tn = 1792
    return pl.pallas_call(
        _in_proj_kernel,
        grid=(n // tm, wn // tn),
        in_specs=[pl.BlockSpec((tm, d), lambda i, j: (i, 0)),
                  pl.BlockSpec((1, d), lambda i, j: (0, 0)),
                  pl.BlockSpec((d, tn), lambda i, j: (0, j))],
        out_specs=pl.BlockSpec((tm, tn), lambda i, j: (i, j)),
        out_shape=jax.ShapeDtypeStruct((n, wn), F32),
        scratch_shapes=[pltpu.VMEM((tm, d), BF16)],
        compiler_params=_cparams("parallel", "arbitrary"),
        name="in_proj",
    )(x, g, w)


def _mlstm_kernel(qkvo_ref, sm_ref, bias_ref, norm_ref, out_ref, cst_ref, mst_ref, c_scr, m_scr):
    j = pl.program_id(1)
    L = qkvo_ref.shape[0]

    @pl.when(j == 0)
    def _():
        c_scr[...] = jnp.zeros_like(c_scr)
        m_scr[...] = jnp.zeros_like(m_scr)

    pre = sm_ref[...] + bias_ref[...]
    causal, anti = _tri(L)
    tril = causal.astype(F32)
    triu = anti.astype(F32)
    bcum = _dot_exact(tril, _log_sigmoid(pre))
    pre_t = pre.T
    ig_t = pre_t[SM_IG:SM_IG + 8, :]
    b_t = _dot_exact(_log_sigmoid(pre_t[SM_FG:SM_FG + 8, :]), triu)
    k_t = qkvo_ref[:, 256:512].T
    lane = lax.broadcasted_iota(jnp.int32, (L, 128), 1)
    sub = lax.broadcasted_iota(jnp.int32, (128, L), 0)

    scs, qcs, cps = [], [], []
    for h in range(HEADS):
        pair, half = divmod(h, 2)
        qq = qkvo_ref[:, 128 * pair:128 * pair + 128]
        kk = qkvo_ref[:, 256 + 128 * pair:256 + 128 * pair + 128]
        inhalf = (lane >= 64 * half) & (lane < 64 * half + 64)
        qm = jnp.where(inhalf, qq, 0.0).astype(BF16)
        cp = c_scr[h]
        cps.append(cp)
        scs.append(_dot_nt(qm, kk.astype(BF16)) * 0.125)
        qcs.append(_dot(qm, cp.astype(BF16)))

    hn_pairs = [jnp.zeros((L, 128), F32), jnp.zeros((L, 128), F32)]
    for h in range(HEADS):
        pair, half = divmod(h, 2)
        lo = 64 * half
        vv = qkvo_ref[:, 512 + 128 * pair:512 + 128 * pair + 128]
        bcol = bcum[:, SM_FG + h:SM_FG + h + 1]
        brow = b_t[h:h + 1, :]
        igrow = ig_t[h:h + 1, :]
        dlog = jnp.where(causal, bcol - brow + igrow, NEG_INF)
        m_h = m_scr[h:h + 1, 0:1]
        inter = bcol + m_h
        mt = jnp.maximum(jnp.max(dlog, -1, keepdims=True), inter)
        s = jnp.exp(dlog - mt) * scs[h]
        g = jnp.exp(inter - mt)
        vsh = vv if half == 0 else pltpu.roll(vv, 64, 1)
        vaug = jnp.where(lane < 64, vsh, jnp.where(lane == 64, 1.0, 0.0)).astype(BF16)
        nd = _dot(s.astype(BF16), vaug) + g * qcs[h]
        den = nd[:, 64:65]
        hh = nd / jnp.maximum(jnp.abs(den), jnp.exp(-mt))
        valid = lane < 64
        mu = jnp.sum(jnp.where(valid, hh, 0.0), -1, keepdims=True) * (1.0 / 64)
        dv = jnp.where(valid, hh - mu, 0.0)
        var = jnp.sum(dv * dv, -1, keepdims=True) * (1.0 / 64)
        hn = dv * lax.rsqrt(var + EPS)
        hn_pairs[pair] = hn_pairs[pair] + (hn if half == 0 else pltpu.roll(hn, 64, 1))
        bl = bcol[L - 1:L, :]
        wlog = bl - brow + igrow
        m_new = jnp.maximum(bl + m_h, jnp.max(wlog, -1, keepdims=True))
        w = jnp.exp(wlog - m_new)
        decay = jnp.exp(bl + m_h - m_new)
        inrows = (sub >= lo) & (sub < lo + 64)
        ktw = jnp.where(inrows, k_t[128 * pair:128 * pair + 128, :], 0.0) * (w * 0.125)
        c_scr[h] = decay * cps[h] + _dot(ktw.astype(BF16), vaug)
        m_scr[h:h + 1, :] = jnp.broadcast_to(m_new, (1, 128))
    for pair in range(2):
        og = qkvo_ref[:, 768 + 128 * pair:768 + 128 * pair + 128]
        out_ref[:, 128 * pair:128 * pair + 128] = (
            jax.nn.sigmoid(og) * (hn_pairs[pair] * norm_ref[:, 128 * pair:128 * pair + 128]))

    @pl.when(j == pl.num_programs(1) - 1)
    def _():
        cst_ref[0] = c_scr[...]
        mst_ref[0] = m_scr[...]


def _mlstm_prompt(u, bsz, t_len, bias_row, norm_row):
    L = min(CHUNK, t_len)
    nc = t_len // L
    n = bsz * t_len
    return pl.pallas_call(
        _mlstm_kernel,
        grid=(bsz, nc),
        in_specs=[pl.BlockSpec((L, 1024), lambda b, j: (b * nc + j, U_MLSTM // 1024)),
                  pl.BlockSpec((L, 128), lambda b, j: (b * nc + j, U_SMALL // 128)),
                  pl.BlockSpec((1, 128), lambda b, j: (0, 0)),
                  pl.BlockSpec((1, WIDTH), lambda b, j: (0, 0))],
        out_specs=[pl.BlockSpec((L, WIDTH), lambda b, j: (b * nc + j, 0)),
                   pl.BlockSpec((1, HEADS, 128, 128), lambda b, j: (b, 0, 0, 0)),
                   pl.BlockSpec((1, 8, 128), lambda b, j: (b, 0, 0))],
        out_shape=[jax.ShapeDtypeStruct((n, WIDTH), F32),
                   jax.ShapeDtypeStruct((bsz, HEADS, 128, 128), F32),
                   jax.ShapeDtypeStruct((bsz, 8, 128), F32)],
        scratch_shapes=[pltpu.VMEM((HEADS, 128, 128), F32), pltpu.VMEM((8, 128), F32)],
        compiler_params=_cparams("parallel", "arbitrary"),
        name="mlstm_prompt",
    )(u, u, bias_row, norm_row)


def _pool_kernel(xb_ref, w_ref, scale_ref, out_ref, st_ref, xs):
    j = pl.program_id(1)
    tt = xb_ref.shape[0]

    @pl.when(j == 0)
    def _():
        xs[0:16, :] = jnp.zeros((16, WIDTH), F32)

    xb = xb_ref[...]
    xs[16:16 + tt, :] = xb
    lane = lax.broadcasted_iota(jnp.int32, (tt, WIDTH), 1)
    pos = j * tt + lax.broadcasted_iota(jnp.int32, (tt, WIDTH), 0)
    acc = xb
    tot = jnp.zeros((tt, WIDTH), F32)
    cnt = jnp.zeros((tt, WIDTH), F32)
    for k in range(1, 16):
        acc = acc + xs[pl.ds(16 - k, tt), :]
        if k + 1 in POOL_WINDOWS:
            g = POOL_WINDOWS.index(k + 1)
            sel = (lane >= 64 * g) & (lane < 64 * g + 64)
            tot = jnp.where(sel, acc, tot)
            cnt = jnp.where(sel, jnp.minimum(pos + 1, k + 1).astype(F32), cnt)
    pooled = tot / cnt - xb
    out_ref[...] = _dot(pooled.astype(BF16), w_ref[...]) * scale_ref[...]
    xs[0:16, :] = xs[tt:tt + 16, :]

    @pl.when(j == pl.num_programs(1) - 1)
    def _():
        st_ref[0] = xs[1:16, :]


def _pool_prompt(u, bsz, t_len, w_blk, scale_row):
    tt = min(512, t_len)
    nt = t_len // tt
    n = bsz * t_len
    return pl.pallas_call(
        _pool_kernel,
        grid=(bsz, nt),
        in_specs=[pl.BlockSpec((tt, WIDTH), lambda b, j: (b * nt + j, U_XB // WIDTH)),
                  pl.BlockSpec((WIDTH, WIDTH), lambda b, j: (0, 0)),
                  pl.BlockSpec((1, WIDTH), lambda b, j: (0, 0))],
        out_specs=[pl.BlockSpec((tt, WIDTH), lambda b, j: (b * nt + j, 0)),
                   pl.BlockSpec((1, POOL_BUF, WIDTH), lambda b, j: (b, 0, 0))],
        out_shape=[jax.ShapeDtypeStruct((n, WIDTH), F32),
                   jax.ShapeDtypeStruct((bsz, POOL_BUF, WIDTH), F32)],
        scratch_shapes=[pltpu.VMEM((tt + 16, WIDTH), F32)],
        compiler_params=_cparams("parallel", "arbitrary"),
        name="pool_prompt",
    )(u, w_blk, scale_row)


def _ssd_kernel(dxbc_ref, dz_ref, sm_ref, cw_ref, cb_ref, bias_ref, alog_row_ref, alog_col_ref,
                skip_ref, norm_ref, out_ref, cst_ref, hst_ref, xs, hs):
    j = pl.program_id(1)
    L = dxbc_ref.shape[0]

    @pl.when(j == 0)
    def _():
        xs[0:8, :] = jnp.zeros((8, D_CONV_DIM), F32)
        hs[...] = jnp.zeros_like(hs)

    xs[8:8 + L, :] = dxbc_ref[...]
    conv = cb_ref[...] + xs[pl.ds(5, L), :] * cw_ref[0:1, :]
    for tap in range(1, D_CONV):
        conv = conv + xs[pl.ds(5 + tap, L), :] * cw_ref[tap:tap + 1, :]
    conv = _silu(conv)
    xs[0:8, :] = xs[L:L + 8, :]
    x4 = conv[:, 0:256]

    pre = sm_ref[...] + bias_ref[...]
    dtf = _softplus(pre)
    causal, anti = _tri(L)
    cum = _dot_exact(causal.astype(F32), dtf * (-jnp.exp(alog_row_ref[...])))
    dt_t = _softplus(pre.T[SM_DT:SM_DT + 8, :])
    cum_t = _dot_exact(dt_t * (-jnp.exp(alog_col_ref[...])), anti.astype(F32))

    lane = lax.broadcasted_iota(jnp.int32, (L, WIDTH), 1)
    dtb = jnp.zeros((L, WIDTH), F32)
    cumb = jnp.zeros((L, WIDTH), F32)
    for h in range(HEADS):
        sel = (lane >= 64 * h) & (lane < 64 * h + 64)
        dtb = jnp.where(sel, dtf[:, SM_DT + h:SM_DT + h + 1], dtb)
        cumb = jnp.where(sel, cum[:, SM_DT + h:SM_DT + h + 1], cumb)
    xdt = (x4 * dtb).astype(BF16)

    y = jnp.zeros((L, WIDTH), F32)
    y_inter = []
    for g in range(2):
        cg = conv[:, 512 + 128 * g:512 + 128 * g + 128].astype(BF16)
        bg = conv[:, 256 + 128 * g:256 + 128 * g + 128].astype(BF16)
        cb = _dot_nt(cg, bg)
        for hl in range(2):
            h = 2 * g + hl
            ccol = cum[:, SM_DT + h:SM_DT + h + 1]
            crow = cum_t[h:h + 1, :]
            seg = jnp.exp(jnp.where(causal, ccol - crow, NEG_INF))
            yh = _dot((cb * seg).astype(BF16), xdt)
            y = jnp.where((lane >= 64 * h) & (lane < 64 * h + 64), yh, y)
        y_inter.append(_dot_nt(cg, hs[128 * g:128 * g + 128, :].astype(BF16)))
    y = y + jnp.concatenate(y_inter, axis=1) * jnp.exp(cumb) + skip_ref[...] * x4
    y = y * _silu(dz_ref[...])
    for g in range(2):
        yg = y[:, 128 * g:128 * g + 128]
        yg = yg * lax.rsqrt(jnp.mean(yg * yg, -1, keepdims=True) + EPS)
        out_ref[:, 128 * g:128 * g + 128] = yg * norm_ref[:, 128 * g:128 * g + 128]

    x4_t = x4.T
    sub = lax.broadcasted_iota(jnp.int32, (128, L), 0)
    subc = lax.broadcasted_iota(jnp.int32, (128, 1), 0)
    for g in range(2):
        bg = conv[:, 256 + 128 * g:256 + 128 * g + 128].astype(BF16)
        rows, decs = [], []
        for hl in range(2):
            h = 2 * g + hl
            last = cum_t[h:h + 1, L - 1:L]
            rows.append(dt_t[h:h + 1, :] * jnp.exp(last - cum_t[h:h + 1, :]))
            decs.append(jnp.exp(last))
        x2t = x4_t[128 * g:128 * g + 128, :] * jnp.where(sub < 64, rows[0], rows[1])
        dec = jnp.where(subc < 64, decs[0], decs[1])
        hs[128 * g:128 * g + 128, :] = hs[128 * g:128 * g + 128, :] * dec + _dot(x2t.astype(BF16), bg)

    @pl.when(j == pl.num_programs(1) - 1)
    def _():
        cst_ref[0] = xs[5:8, :]
        hst_ref[0] = hs[...]


def _ssd_prompt(u, bsz, t_len, cw, cb, bias_row, alog_row, alog_col, skip_row, norm_row):
    L = min(CHUNK, t_len)
    nc = t_len // L
    n = bsz * t_len
    const = lambda shape: pl.BlockSpec(shape, lambda b, j: (0,) * len(shape))
    return pl.pallas_call(
        _ssd_kernel,
        grid=(bsz, nc),
        in_specs=[pl.BlockSpec((L, D_CONV_DIM), lambda b, j: (b * nc + j, U_DXBC // D_CONV_DIM)),
                  pl.BlockSpec((L, WIDTH), lambda b, j: (b * nc + j, U_DZ // WIDTH)),
                  pl.BlockSpec((L, 128), lambda b, j: (b * nc + j, U_SMALL // 128)),
                  const((D_CONV, D_CONV_DIM)), const((1, D_CONV_DIM)), const((1, 128)), const((1, 128)),
                  const((8, 1)), const((1, WIDTH)), const((1, WIDTH))],
        out_specs=[pl.BlockSpec((L, WIDTH), lambda b, j: (b * nc + j, 0)),
                   pl.BlockSpec((1, D_CONV - 1, D_CONV_DIM), lambda b, j: (b, 0, 0)),
                   pl.BlockSpec((1, HEADS * DH, D_STATE), lambda b, j: (b, 0, 0))],
        out_shape=[jax.ShapeDtypeStruct((n, WIDTH), F32),
                   jax.ShapeDtypeStruct((bsz, D_CONV - 1, D_CONV_DIM), F32),
                   jax.ShapeDtypeStruct((bsz, HEADS * DH, D_STATE), F32)],
        scratch_shapes=[pltpu.VMEM((L + 8, D_CONV_DIM), F32), pltpu.VMEM((HEADS * DH, D_STATE), F32)],
        compiler_params=_cparams("parallel", "arbitrary"),
        name="ssd_prompt",
    )(u, u, u, cw, cb, bias_row, alog_row, alog_col, skip_row, norm_row)


def _mla_prep_kernel(cq_ref, ckv_ref, k1_ref, k2_ref, cs1_ref, cs2_ref, cs1t_ref, cs2t_ref, csa_ref, csb_ref,
                     qn_ref, kvn_ref, wnt_ref, wx1t_ref, wx2t_ref, wukt_ref, hmt_ref,
                     clat_ref, kr_ref, kcat_ref, clatt_ref, qt_ref):
    ql_t = _rms(cq_ref[...], qn_ref[...]).T.astype(BF16)
    qnope_t = _dot(wnt_ref[...], ql_t)
    tail_t = _dot(wx1t_ref[...], ql_t) * cs1t_ref[...] + _dot(wx2t_ref[...], ql_t) * cs2t_ref[...]
    qabs_t = _dot(wukt_ref[...], qnope_t.astype(BF16))
    for h in range(HEADS):
        qt_ref[h, 0:128, :] = qabs_t[128 * h:128 * h + 128, :].astype(BF16)
        qt_ref[h, 128:256, :] = (tail_t * hmt_ref[h]).astype(BF16)
    clat = _rms(ckv_ref[...], kvn_ref[...])
    clat_ref[...] = clat
    clatt_ref[0] = clat.T.astype(BF16)
    k1 = k1_ref[...]
    k2 = k2_ref[...]
    kcat_ref[:, 0:128] = clat.astype(BF16)
    kcat_ref[:, 128:256] = (k1 * cs1_ref[...] + k2 * cs2_ref[...]).astype(BF16)
    kr_ref[...] = k1[:, 0:C_ROPE] * csa_ref[...] + k2[:, 0:C_ROPE] * csb_ref[...]


def _mla_prep(u, t_len, tabs, qn, kvn, wnt, wx1t, wx2t, wukt, hmt):
    n = u.shape[0]
    tm = min(CHUNK, t_len)
    ntab = t_len // tm
    tab = lambda w: pl.BlockSpec((tm, w), lambda i: (i % ntab, 0))
    tabt = pl.BlockSpec((128, tm), lambda i: (0, i % ntab))
    const = lambda shape: pl.BlockSpec(shape, lambda i: (0,) * len(shape))
    col = lambda w, off: pl.BlockSpec((tm, w), lambda i: (i, off // w))
    return pl.pallas_call(
        _mla_prep_kernel,
        grid=(n // tm,),
        in_specs=[col(256, U_CQ), col(128, U_CKV), col(128, U_K1), col(128, U_K2),
                  tab(128), tab(128), tabt, tabt, tab(C_ROPE), tab(C_ROPE),
                  const((1, 256)), const((1, 128)), const((256, 256)), const((128, 256)), const((128, 256)),
                  const((512, 256)), const((HEADS, 128, 1))],
        out_specs=[pl.BlockSpec((tm, 128), lambda i: (i, 0)),
                   pl.BlockSpec((tm, C_ROPE), lambda i: (i, 0)),
                   pl.BlockSpec((tm, 256), lambda i: (i, 0)),
                   pl.BlockSpec((1, 128, tm), lambda i: (i, 0, 0)),
                   pl.BlockSpec((HEADS, 256, tm), lambda i: (i, 0, 0))],
        out_shape=[jax.ShapeDtypeStruct((n, 128), F32),
                   jax.ShapeDtypeStruct((n, C_ROPE), F32),
                   jax.ShapeDtypeStruct((n, 256), BF16),
                   jax.ShapeDtypeStruct((n // tm, 128, tm), BF16),
                   jax.ShapeDtypeStruct((n // tm * HEADS, 256, tm), BF16)],
        compiler_params=_cparams("parallel"),
        name="mla_prep",
    )(u, u, u, u, *tabs, qn, kvn, wnt, wx1t, wx2t, wukt, hmt)


def _flash_kernel(qt_ref, k_ref, ct_ref, wuvt_ref, o_ref):
    qi = pl.program_id(1)
    tq = qt_ref.shape[2]
    key_idx = lax.broadcasted_iota(jnp.int32, (tq, tq), 0)
    qry_idx = lax.broadcasted_iota(jnp.int32, (tq, tq), 1)
    ahead = 2

    def scores(kb, h):
        k = k_ref[pl.ds(pl.multiple_of(kb * tq, tq), tq), :]
        return _dot(k, qt_ref[h]) * MLA_SCALE

    def step(kb, carry, last):
        state, s_in = carry
        c_t = ct_ref[kb]
        s_all = list(s_in)
        s_out = []
        out = []
        for h in range(HEADS):
            if h + ahead < HEADS:
                s_all.append(scores(kb, h + ahead))
            elif not last:
                s_out.append(scores(kb + 1, h + ahead - HEADS))
            m_prev, l_prev, acc = state[h]
            s = jnp.where(key_idx <= qry_idx, s_all[h], NEG_INF) if last else s_all[h]
            m_new = jnp.maximum(m_prev, jnp.max(s, axis=0, keepdims=True))
            alpha = jnp.exp(m_prev - m_new)
            p = jnp.exp(s - m_new)
            l_new = alpha * l_prev + jnp.sum(p, axis=0, keepdims=True)
            out.append((m_new, l_new, alpha * acc + _dot(c_t, p.astype(BF16))))
        return tuple(out), tuple(s_out)

    init = tuple((jnp.full((1, tq), NEG_INF, F32), jnp.zeros((1, tq), F32), jnp.zeros((128, tq), F32))
                 for _ in range(HEADS))
    carry = (init, tuple(scores(0, h) for h in range(ahead)))
    carry = lax.fori_loop(0, qi, functools.partial(step, last=False), carry)
    fin, _ = step(qi, carry, True)
    parts = []
    for h in range(HEADS):
        _, l_fin, acc_fin = fin[h]
        o_t = (acc_fin / l_fin).astype(BF16)
        parts.append(_dot(wuvt_ref[64 * h:64 * h + 64, :], o_t))
    o_ref[...] = jnp.concatenate(parts, axis=0).T


def _mla_prompt(q_t, kcat, clat_t, wuv_t, bsz, t_len):
    tq = min(CHUNK, t_len)
    nq = t_len // tq
    n = bsz * t_len
    return pl.pallas_call(
        _flash_kernel,
        grid=(bsz, nq),
        in_specs=[pl.BlockSpec((HEADS, 256, tq), lambda b, i: (b * nq + i, 0, 0)),
                  pl.BlockSpec((t_len, 256), lambda b, i: (b, 0)),
                  pl.BlockSpec((nq, 128, tq), lambda b, i: (b, 0, 0)),
                  pl.BlockSpec((WIDTH, 128), lambda b, i: (0, 0))],
        out_specs=pl.BlockSpec((tq, WIDTH), lambda b, i: (b * nq + i, 0)),
        out_shape=jax.ShapeDtypeStruct((n, WIDTH), F32),
        compiler_params=_cparams("parallel", "arbitrary"),
        name="mla_prompt",
    )(q_t, kcat, clat_t, wuv_t)


def _decode_kernel(pt_ref, q_ref, kn_ref, wuv_ref, ckv_hbm, kr_hbm, o_ref, *scratch, layer, n_pages, cp):
    bufc, bufr, sem = scratch[0:DECODE_SLOTS], scratch[DECODE_SLOTS:2 * DECODE_SLOTS], scratch[2 * DECODE_SLOTS]
    b = pl.program_id(0)
    nb = pl.num_programs(0)
    n_chunks = n_pages // cp
    n_total = nb * n_chunks

    def copies(g, slot, p):
        page = pt_ref[g * cp + p]
        return (pltpu.make_async_copy(ckv_hbm.at[layer, page], bufc[slot].at[p], sem.at[0, slot]),
                pltpu.make_async_copy(kr_hbm.at[layer, page], bufr[slot].at[p], sem.at[1, slot]))

    def start_chunk(g, slot):
        for p in range(cp):
            for cpy in copies(g, slot, p):
                cpy.start()

    def wait_chunk(g, slot):
        for p in range(cp):
            for cpy in copies(g, slot, p):
                cpy.wait()

    @pl.when(b == 0)
    def _():
        start_chunk(0, 0)
        start_chunk(1, 1)

    q = q_ref[0]
    q_lat = q[:, 0:C_KV_RANK]
    q_rope = q[:, C_KV_RANK:C_KV_RANK + C_ROPE]
    kn = kn_ref[0].astype(BF16).astype(F32)
    s_new = jnp.sum(q.astype(F32) * kn, -1, keepdims=True) * MLA_SCALE
    m0 = s_new
    l0 = jnp.ones((8, 1), F32)
    acc0 = jnp.broadcast_to(kn[:, 0:128], (8, 128))

    def scores(slot):
        kc = bufc[slot][...].reshape(cp * 128, 128).astype(BF16)
        s_rope = [_dot(q_rope, bufr[slot][p].astype(BF16)) for p in range(cp)]
        return kc, (_dot_nt(q_lat, kc) + jnp.concatenate(s_rope, axis=1)) * MLA_SCALE

    def update(carry, kc, s):
        m_prev, l_prev, acc = carry
        m_new = jnp.maximum(m_prev, jnp.max(s, -1, keepdims=True))
        alpha = jnp.exp(m_prev - m_new)
        p = jnp.exp(s - m_new)
        return m_new, alpha * l_prev + jnp.sum(p, -1, keepdims=True), alpha * acc + _dot(p.astype(BF16), kc)

    def pair(g, slot, carry):
        other = (slot + 2) % DECODE_SLOTS
        wait_chunk(g, slot)
        wait_chunk(g + 1, slot + 1)
        start_chunk(jnp.minimum(g + 2, n_total - 1), other)
        start_chunk(jnp.minimum(g + 3, n_total - 1), other + 1)
        kc_a, s_a = scores(slot)
        kc_b, s_b = scores(slot + 1)
        return update(update(carry, kc_a, s_a), kc_b, s_b)

    def body(it, carry):
        g = b * n_chunks + DECODE_SLOTS * it
        return pair(g + 2, 2, pair(g, 0, carry))

    _, l_fin, acc_fin = lax.fori_loop(0, n_chunks // DECODE_SLOTS, body, (m0, l0, acc0))

    @pl.when(b == nb - 1)
    def _():
        wait_chunk(n_total - 1, 0)
        wait_chunk(n_total - 1, 1)

    o = (acc_fin / l_fin).astype(BF16)
    of = _dot(o, wuv_ref[...])
    row = lax.broadcasted_iota(jnp.int32, (8, WIDTH), 0)
    lane = lax.broadcasted_iota(jnp.int32, (8, WIDTH), 1)
    diag = (lane >= 64 * row) & (lane < 64 * row + 64)
    o_ref[0] = jnp.sum(jnp.where(diag, of, 0.0), axis=0, keepdims=True)


def _mla_decode(page_table, q8, k_new, wuv, cache_ckv, cache_kr, layer):
    bsz, n_pages = page_table.shape
    cp = min(PAGES_PER_STEP, n_pages // DECODE_SLOTS)
    assert n_pages % (cp * DECODE_SLOTS) == 0
    page = cache_ckv.shape[2]
    kern = functools.partial(_decode_kernel, layer=layer, n_pages=n_pages, cp=cp)
    grid_spec = pltpu.PrefetchScalarGridSpec(
        num_scalar_prefetch=1,
        grid=(bsz,),
        in_specs=[pl.BlockSpec((1, 8, 256), lambda b, pt: (b, 0, 0)),
                  pl.BlockSpec((1, 1, 256), lambda b, pt: (b, 0, 0)),
                  pl.BlockSpec((128, WIDTH), lambda b, pt: (0, 0)),
                  pl.BlockSpec(memory_space=pl.ANY),
                  pl.BlockSpec(memory_space=pl.ANY)],
        out_specs=pl.BlockSpec((1, 1, WIDTH), lambda b, pt: (b, 0, 0)),
        scratch_shapes=[pltpu.VMEM((cp, page, 128), F32)] * DECODE_SLOTS
                       + [pltpu.VMEM((cp, C_ROPE, page), F32)] * DECODE_SLOTS
                       + [pltpu.SemaphoreType.DMA((2, DECODE_SLOTS))],
    )
    return pl.pallas_call(
        kern,
        grid_spec=grid_spec,
        out_shape=jax.ShapeDtypeStruct((bsz, 1, WIDTH), F32),
        compiler_params=_cparams("arbitrary"),
        name="mla_decode",
    )(page_table.reshape(-1), q8, k_new, wuv, cache_ckv, cache_kr)


def _mlstm_step_kernel(qr_ref, kr_ref, vr_ref, or_ref, qc_ref, kc_ref, ig_ref, fg_ref, c_ref, n_ref, m_ref,
                       bi_ref, bf_ref, norm_ref, out_ref, c1_ref, n1_ref, m1_ref):
    q = qr_ref[...]
    k = kr_ref[...] * 0.125
    ig = ig_ref[...] + bi_ref[...]
    lf = _log_sigmoid(fg_ref[...] + bf_ref[...])
    m0 = m_ref[...]
    c0 = c_ref[...]
    n0 = n_ref[...]
    inter = lf + m0
    mt = jnp.maximum(ig, inter)
    s = jnp.exp(ig - mt) * jnp.sum(q * k, -1, keepdims=True)
    g = jnp.exp(inter - mt)
    qc = jnp.sum(qc_ref[...] * c0, axis=2, keepdims=True)
    num = s * vr_ref[...] + g * qc
    den = s + g * jnp.sum(q * n0, -1, keepdims=True)
    hh = num / jnp.maximum(jnp.abs(den), jnp.exp(-mt))
    mu = jnp.mean(hh, -1, keepdims=True)
    dv = hh - mu
    var = jnp.mean(dv * dv, -1, keepdims=True)
    hn = dv * lax.rsqrt(var + EPS) * norm_ref[...]
    out_ref[...] = jax.nn.sigmoid(or_ref[...]) * hn
    w = jnp.exp(ig - mt)
    decay = jnp.exp(inter - mt)
    c1_ref[...] = decay * c0 + (w * 0.125 * kc_ref[...]) * vr_ref[...]
    n1_ref[...] = decay * n0 + w * k
    m1_ref[...] = mt


def _mlstm_step(q, k, v, o, ig, fg, c0, n0, m0, bi, bf, norm):
    bsz = q.shape[0]
    bb = min(16, bsz)
    r4 = lambda a: a.reshape(bsz, HEADS, 1, DH)
    c4 = lambda a: a.reshape(bsz, HEADS, DH, 1)
    s4 = lambda a: a.reshape(bsz, HEADS, 1, 1)
    blk = lambda d2, d3: pl.BlockSpec((bb, HEADS, d2, d3), lambda i: (i, 0, 0, 0))
    par = lambda d2, d3: pl.BlockSpec((1, HEADS, d2, d3), lambda i: (0, 0, 0, 0))
    return pl.pallas_call(
        _mlstm_step_kernel,
        grid=(bsz // bb,),
        in_specs=[blk(1, DH)] * 4 + [blk(DH, 1)] * 2 + [blk(1, 1)] * 2
                 + [blk(DH, DH), blk(1, DH), blk(1, 1), par(1, 1), par(1, 1), par(1, DH)],
        out_specs=[blk(1, DH), blk(DH, DH), blk(1, DH), blk(1, 1)],
        out_shape=[jax.ShapeDtypeStruct((bsz, HEADS, 1, DH), F32),
                   jax.ShapeDtypeStruct((bsz, HEADS, DH, DH), F32),
                   jax.ShapeDtypeStruct((bsz, HEADS, 1, DH), F32),
                   jax.ShapeDtypeStruct((bsz, HEADS, 1, 1), F32)],
        compiler_params=_cparams("parallel"),
        name="mlstm_step",
    )(r4(q), r4(k), r4(v), r4(o), c4(q), c4(k), s4(ig), s4(fg), c0, r4(n0), s4(m0),
      bi.reshape(1, HEADS, 1, 1), bf.reshape(1, HEADS, 1, 1), norm.reshape(1, HEADS, 1, DH))


def _pool_step_kernel(xb_ref, st_ref, pooled_ref, st1_ref, *, pos):
    xb = xb_ref[...]
    lane = lax.broadcasted_iota(jnp.int32, xb.shape, 2)
    acc = xb
    pooled = jnp.zeros_like(xb)
    for k in range(1, 16):
        acc = acc + st_ref[:, POOL_BUF - k:POOL_BUF - k + 1, :]
        if k + 1 in POOL_WINDOWS:
            g = POOL_WINDOWS.index(k + 1)
            cnt = float(min(pos + 1, k + 1))
            pooled = jnp.where((lane >= 64 * g) & (lane < 64 * g + 64), acc / cnt - xb, pooled)
    pooled_ref[...] = pooled
    st1_ref[:, 0:POOL_BUF - 1, :] = st_ref[:, 1:POOL_BUF, :]
    st1_ref[:, POOL_BUF - 1:POOL_BUF, :] = xb


def _pool_step(xb, state, pos):
    bsz = xb.shape[0]
    bb = min(32, bsz)
    return pl.pallas_call(
        functools.partial(_pool_step_kernel, pos=pos),
        grid=(bsz // bb,),
        in_specs=[pl.BlockSpec((bb, 1, WIDTH), lambda i: (i, 0, 0)),
                  pl.BlockSpec((bb, POOL_BUF, WIDTH), lambda i: (i, 0, 0))],
        out_specs=[pl.BlockSpec((bb, 1, WIDTH), lambda i: (i, 0, 0)),
                   pl.BlockSpec((bb, POOL_BUF, WIDTH), lambda i: (i, 0, 0))],
        out_shape=[jax.ShapeDtypeStruct((bsz, 1, WIDTH), F32),
                   jax.ShapeDtypeStruct((bsz, POOL_BUF, WIDTH), F32)],
        compiler_params=_cparams("parallel"),
        name="pool_step",
    )(xb.reshape(bsz, 1, WIDTH), state)


def _pool_lin_kernel(p_ref, w_ref, scale_ref, o_ref):
    o_ref[...] = _dot(p_ref[...].astype(BF16), w_ref[...]) * scale_ref[...]


def _pool_lin(pooled, w_blk, scale_row):
    n = pooled.shape[0]
    return pl.pallas_call(
        _pool_lin_kernel,
        out_shape=jax.ShapeDtypeStruct((n, WIDTH), F32),
        name="pool_lin",
    )(pooled, w_blk, scale_row)


def _ssd_step_kernel(x0_ref, x1_ref, x2_ref, x3_ref, bc_ref, st_ref, full_ref, dz_ref, ddt_ref, h_ref,
                     cwx_ref, cbx_ref, cwbc_ref, cbbc_ref, dtb_ref, alog_ref, skip_ref, norm_ref,
                     out_ref, st1_ref, h1_ref):
    xc = cbx_ref[...] + x0_ref[...] * cwx_ref[0] + x1_ref[...] * cwx_ref[1] \
        + x2_ref[...] * cwx_ref[2] + x3_ref[...] * cwx_ref[3]
    xc = _silu(xc)
    bc = cbbc_ref[...] + bc_ref[...] * cwbc_ref[3:4, :]
    for tap in range(D_CONV - 1):
        bc = bc + st_ref[:, tap:tap + 1, 256:768] * cwbc_ref[tap:tap + 1, :]
    bc = _silu(bc)
    dt = _softplus(ddt_ref[...] + dtb_ref[...])
    dec = jnp.exp(dt * (-jnp.exp(alog_ref[...])))
    xdt = xc * dt
    ys = []
    for g in range(2):
        bg = bc[:, :, 128 * g:128 * g + 128][:, None]
        cg = bc[:, :, 256 + 128 * g:256 + 128 * g + 128][:, None]
        h1 = h_ref[:, 2 * g:2 * g + 2] * dec[:, 2 * g:2 * g + 2] + xdt[:, 2 * g:2 * g + 2] * bg
        h1_ref[:, 2 * g:2 * g + 2] = h1
        ys.append(jnp.sum(h1 * cg, -1, keepdims=True))
    for g in range(2):
        y = ys[g] + skip_ref[:, 2 * g:2 * g + 2] * xc[:, 2 * g:2 * g + 2]
        y = y * _silu(dz_ref[:, 2 * g:2 * g + 2])
        ss = jnp.sum(y * y, axis=2, keepdims=True)
        ms = (ss[:, 0:1] + ss[:, 1:2]) * (1.0 / 128)
        out_ref[:, 2 * g:2 * g + 2] = y * lax.rsqrt(ms + EPS) * norm_ref[:, 2 * g:2 * g + 2]
    st1_ref[:, 0:D_CONV - 2, :] = st_ref[:, 1:D_CONV - 1, :]
    st1_ref[:, D_CONV - 2:D_CONV - 1, :] = full_ref[...]


def _ssd_step(dxbc, dz, ddt, state_conv, h0, cw, cb, dt_bias, a_log, skip, norm):
    bsz = dxbc.shape[0]
    bb = min(8, bsz)
    c4 = lambda a: a.reshape(bsz, HEADS, DH, 1)
    b4 = lambda d1, d2, d3: pl.BlockSpec((bb, d1, d2, d3), lambda i: (i, 0, 0, 0))
    b3 = lambda d1, d2: pl.BlockSpec((bb, d1, d2), lambda i: (i, 0, 0))
    par = lambda shape: pl.BlockSpec(shape, lambda i: (0,) * len(shape))
    col = b4(HEADS, DH, 1)
    return pl.pallas_call(
        _ssd_step_kernel,
        grid=(bsz // bb,),
        in_specs=[col, col, col, col, b3(1, 512), b3(D_CONV - 1, D_CONV_DIM), b3(1, D_CONV_DIM), col,
                  b4(HEADS, 1, 1), b4(HEADS, DH, D_STATE),
                  par((D_CONV, HEADS, DH, 1)), par((1, HEADS, DH, 1)), par((D_CONV, 512)), par((1, 512)),
                  par((1, HEADS, 1, 1)), par((1, HEADS, 1, 1)), par((1, HEADS, 1, 1)), par((1, HEADS, DH, 1))],
        out_specs=[col, b3(D_CONV - 1, D_CONV_DIM), b4(HEADS, DH, D_STATE)],
        out_shape=[jax.ShapeDtypeStruct((bsz, HEADS, DH, 1), F32),
                   jax.ShapeDtypeStruct((bsz, D_CONV - 1, D_CONV_DIM), F32),
                   jax.ShapeDtypeStruct((bsz, HEADS, DH, D_STATE), F32)],
        compiler_params=_cparams("parallel"),
        name="ssd_step",
    )(c4(state_conv[:, 0, :256]), c4(state_conv[:, 1, :256]), c4(state_conv[:, 2, :256]), c4(dxbc[:, :256]),
      dxbc[:, 256:].reshape(bsz, 1, 512), state_conv, dxbc.reshape(bsz, 1, D_CONV_DIM), c4(dz),
      ddt.reshape(bsz, HEADS, 1, 1), h0,
      cw[:, :256].reshape(D_CONV, HEADS, DH, 1), cb[:256].reshape(1, HEADS, DH, 1), cw[:, 256:],
      cb[256:].reshape(1, 512), dt_bias.reshape(1, HEADS, 1, 1), a_log.reshape(1, HEADS, 1, 1),
      skip.reshape(1, HEADS, 1, 1), norm.reshape(1, HEADS, DH, 1))


def _mla_prep_s_kernel(cq_ref, ckv_ref, k1_ref, k2_ref, csa_ref, csb_ref, qn_ref, kvn_ref, wn_ref, wxa_ref,
                       wxb_ref, wuk_ref, clat_ref, kr_ref, qa_ref, qr_ref):
    ql = _rms(cq_ref[...], qn_ref[...]).astype(BF16)
    qabs = _dot(_dot(ql, wn_ref[...]).astype(BF16), wuk_ref[...])
    csa = csa_ref[...]
    csb = csb_ref[...]
    for h in range(HEADS):
        qa_ref[h] = qabs[:, 128 * h:128 * h + 128].astype(BF16)
        qr_ref[h] = (_dot(ql, wxa_ref[h]) * csa + _dot(ql, wxb_ref[h]) * csb).astype(BF16)
    clat_ref[...] = _rms(ckv_ref[...], kvn_ref[...])
    kr_ref[...] = k1_ref[:, 0:C_ROPE] * csa + k2_ref[:, 0:C_ROPE] * csb


def _mla_prep_sample(u, csa, csb, qn, kvn, wn, wxa, wxb, wuk):
    n = u.shape[0]
    const = lambda shape: pl.BlockSpec(shape, lambda i: (0,) * len(shape))
    col = lambda w, off: pl.BlockSpec((n, w), lambda i: (0, off // w))
    return pl.pallas_call(
        _mla_prep_s_kernel,
        grid=(1,),
        in_specs=[col(256, U_CQ), col(128, U_CKV), col(128, U_K1), col(128, U_K2),
                  const((1, C_ROPE)), const((1, C_ROPE)), const((1, 256)), const((1, 128)), const((256, 256)),
                  const((HEADS, 256, C_ROPE)), const((HEADS, 256, C_ROPE)), const((256, 512))],
        out_specs=[const((n, 128)), const((n, C_ROPE)), const((HEADS, n, 128)), const((HEADS, n, C_ROPE))],
        out_shape=[jax.ShapeDtypeStruct((n, 128), F32),
                   jax.ShapeDtypeStruct((n, C_ROPE), F32),
                   jax.ShapeDtypeStruct((HEADS, n, 128), BF16),
                   jax.ShapeDtypeStruct((HEADS, n, C_ROPE), BF16)],
        compiler_params=_cparams("arbitrary"),
        name="mla_prep_sample",
    )(u, u, u, u, csa, csb, qn, kvn, wn, wxa, wxb, wuk)


def _merge_kernel(x_ref, oa_ref, ob_ref, oc_ref, od_ref, g0_ref, g1_ref, g2_ref, g3_ref, wb_ref, wo_ref,
                  gn_ref, out_ref):
    mixed = None
    for i, (o_ref, g_ref) in enumerate(((oa_ref, g0_ref), (ob_ref, g1_ref), (oc_ref, g2_ref), (od_ref, g3_ref))):
        term = jax.nn.sigmoid(g_ref[...]) * _dot(o_ref[...].astype(BF16), wb_ref[i])
        mixed = term if mixed is None else mixed + term
    y = _dot(mixed.astype(BF16), wo_ref[...])
    out_ref[...] = x_ref[...] + _rms(y, gn_ref[...])


def _merge(x, outs, u, wb, wo, gn):
    n, d = x.shape
    tm = min(512, n)
    row = lambda w: pl.BlockSpec((tm, w), lambda i: (i, 0))
    gate = lambda k: pl.BlockSpec((tm, d), lambda i: (i, U_GATES // d + k))
    return pl.pallas_call(
        _merge_kernel,
        grid=(n // tm,),
        in_specs=[row(d)] + [row(WIDTH)] * 4 + [gate(k) for k in range(4)]
                 + [pl.BlockSpec((4, WIDTH, d), lambda i: (0, 0, 0)), pl.BlockSpec((d, d), lambda i: (0, 0)),
                    pl.BlockSpec((1, d), lambda i: (0, 0))],
        out_specs=row(d),
        out_shape=jax.ShapeDtypeStruct((n, d), F32),
        compiler_params=_cparams("parallel"),
        name="merge",
    )(x, *outs, u, u, u, u, wb, wo, gn)


def _ffn_kernel(x_ref, g1_ref, wg_ref, wu_ref, wd_ref, g2_ref, out_ref, xn_scr, acc_scr):
    k = pl.program_id(1)

    @pl.when(k == 0)
    def _():
        xn_scr[...] = _rms(x_ref[...], g1_ref[...]).astype(BF16)
        acc_scr[...] = jnp.zeros_like(acc_scr)

    xn = xn_scr[...]
    hact = _silu(_dot(xn, wg_ref[...])) * _dot(xn, wu_ref[...])
    acc_scr[...] += _dot(hact.astype(BF16), wd_ref[...])

    @pl.when(k == pl.num_programs(1) - 1)
    def _():
        out_ref[...] = x_ref[...] + _rms(acc_scr[...], g2_ref[...])


def _ffn(x, g1, wgu, wd, g2):
    n, d = x.shape
    hid = wd.shape[0]
    tm = min(512, n)
    nk = 2
    th = hid // nk
    return pl.pallas_call(
        _ffn_kernel,
        grid=(n // tm, nk),
        in_specs=[pl.BlockSpec((tm, d), lambda i, k: (i, 0)),
                  pl.BlockSpec((1, d), lambda i, k: (0, 0)),
                  pl.BlockSpec((d, th), lambda i, k: (0, k)),
                  pl.BlockSpec((d, th), lambda i, k: (0, k + nk)),
                  pl.BlockSpec((th, d), lambda i, k: (k, 0)),
                  pl.BlockSpec((1, d), lambda i, k: (0, 0))],
        out_specs=pl.BlockSpec((tm, d), lambda i, k: (i, 0)),
        out_shape=jax.ShapeDtypeStruct((n, d), F32),
        scratch_shapes=[pltpu.VMEM((tm, d), BF16), pltpu.VMEM((tm, d), F32)],
        compiler_params=_cparams("parallel", "arbitrary"),
        name="ffn",
    )(x, g1, wgu, wgu, wd, g2)


def _lane_row(width, pieces):
    row = jnp.zeros((1, width), F32)
    for off, vals in pieces:
        row = row.at[0, off:off + vals.shape[0]].set(vals.astype(F32))
    return row


def _layer_params(l, n_mix_pre, n_mix_post, n_ffn_pre, n_ffn_post, w_in, a_bi, a_bf, a_norm, b_w, b_scale,
                  c_qnorm, c_wuq, c_kvnorm, c_wuk, c_wuv, d_conv_w, d_conv_b, d_dt_bias, d_a_log, d_skip, d_norm,
                  w_branch, w_o, w_gu, w_down):
    offs = np.concatenate([[0], np.cumsum(IN_SPLITS)]).tolist()
    seg = [w_in[l][:, offs[i]:offs[i + 1]] for i in range(len(IN_SPLITS))]
    aq, ak, av, ao, ai, af, xb, cq, ckv, ckr, dz, dxbc, ddt, gates = seg
    z = lambda w: jnp.zeros((D_MODEL, w), F32)
    small = jnp.concatenate([ckr, ai, z(4), af, z(4), ddt, z(128 - SM_DT - 4)], 1)
    k1 = jnp.tile(ckr[:, :16], (1, 8))
    k2 = jnp.tile(ckr[:, 16:], (1, 8))
    p = {}
    p["w_in"] = jnp.concatenate([aq, ak, av, ao, xb, cq, dxbc, dz, ckv, small, k1, k2, gates], 1).astype(BF16)
    p["n_mix_pre"] = n_mix_pre[l][None]
    p["n_mix_post"] = n_mix_post[l][None]
    p["n_ffn_pre"] = n_ffn_pre[l][None]
    p["n_ffn_post"] = n_ffn_post[l][None]
    p["a_bias_row"] = _lane_row(128, [(SM_IG, a_bi[l]), (SM_FG, a_bf[l])])
    p["a_norm_row"] = a_norm[l][None]
    p["a_bi"], p["a_bf"], p["a_norm"] = a_bi[l], a_bf[l], a_norm[l]
    wblk = jnp.zeros((WIDTH, WIDTH), F32)
    for g in range(4):
        wblk = wblk.at[64 * g:64 * g + 64, 64 * g:64 * g + 64].set(b_w[l][g])
    p["b_wblk"] = wblk.astype(BF16)
    p["b_scale_row"] = b_scale[l][None]
    wuq = c_wuq[l].reshape(256, HEADS, 96)
    p["c_wn"] = wuq[:, :, :64].reshape(256, 256).astype(BF16)
    p["c_wn_t"] = p["c_wn"].T
    wx1 = wuq[:, :, 64:80]
    wx2 = wuq[:, :, 80:96]
    p["c_wx1_t"] = jnp.tile(wx1.reshape(256, 64), (1, 2)).astype(BF16).T
    p["c_wx2_t"] = jnp.tile(wx2.reshape(256, 64), (1, 2)).astype(BF16).T
    p["c_wxa"] = jnp.concatenate([wx1, wx1], -1).transpose(1, 0, 2).astype(BF16)
    p["c_wxb"] = jnp.concatenate([wx2, wx2], -1).transpose(1, 0, 2).astype(BF16)
    wuk_bd = jnp.zeros((256, 512), F32)
    for h in range(HEADS):
        wuk_bd = wuk_bd.at[64 * h:64 * h + 64, 128 * h:128 * h + 128].set(c_wuk[l][:, h, :].T)
    p["c_wuk_bd"] = wuk_bd.astype(BF16)
    p["c_wuk_bd_t"] = p["c_wuk_bd"].T
    p["c_wuv"] = c_wuv[l].reshape(C_KV_RANK, WIDTH).astype(BF16)
    p["c_wuv_t"] = p["c_wuv"].T
    p["c_qnorm"] = c_qnorm[l][None]
    p["c_kvnorm"] = c_kvnorm[l][None]
    p["d_conv_w"] = d_conv_w[l]
    p["d_conv_b"] = d_conv_b[l]
    p["d_conv_b_row"] = d_conv_b[l][None]
    p["d_bias_row"] = _lane_row(128, [(SM_DT, d_dt_bias[l])])
    p["d_alog_row"] = _lane_row(128, [(SM_DT, d_a_log[l])])
    p["d_alog_col"] = _lane_row(8, [(0, d_a_log[l])]).reshape(8, 1)
    p["d_skip_row"] = jnp.repeat(d_skip[l], DH)[None]
    p["d_norm_row"] = d_norm[l][None]
    p["d_dt_bias"], p["d_a_log"], p["d_skip"], p["d_norm"] = d_dt_bias[l], d_a_log[l], d_skip[l], d_norm[l]
    p["w_branch"] = w_branch[l].astype(BF16)
    p["w_o"] = w_o[l].astype(BF16)
    p["w_gu"] = w_gu[l].astype(BF16)
    p["w_down"] = w_down[l].astype(BF16)
    return p


def _rope_tables(pos):
    inv = ROPE_THETA ** (-jnp.arange(16, dtype=F32) / 16)
    ang = pos.astype(F32)[:, None] * inv[None, :]
    cos, sin = jnp.cos(ang), jnp.sin(ang)
    cos64, sin64 = jnp.tile(cos, (1, 4)), jnp.tile(sin, (1, 4))
    cs1 = jnp.concatenate([cos64, sin64], 1)
    cs2 = jnp.concatenate([-sin64, cos64], 1)
    csa = jnp.concatenate([cos, sin], 1)
    csb = jnp.concatenate([-sin, cos], 1)
    return cs1, cs2, cs1.T, cs2.T, csa, csb


def _head_lane_mask():
    lane = np.arange(128)
    return jnp.asarray(((lane % 64) // 16)[None, :] == np.arange(HEADS)[:, None], F32)[:, :, None]


def _prompt_layer(x, bsz, t_len, p, tabs, hm):
    u = _in_proj(x, p["n_mix_pre"], p["w_in"])
    out_a, c_pad, m_pad = _mlstm_prompt(u, bsz, t_len, p["a_bias_row"], p["a_norm_row"])
    out_b, pool1 = _pool_prompt(u, bsz, t_len, p["b_wblk"], p["b_scale_row"])
    out_d, conv1, ssm1 = _ssd_prompt(u, bsz, t_len, p["d_conv_w"], p["d_conv_b_row"], p["d_bias_row"],
                                     p["d_alog_row"], p["d_alog_col"], p["d_skip_row"], p["d_norm_row"])
    c_lat, k_rope, kcat, clat_t, q_t = _mla_prep(u, t_len, tabs, p["c_qnorm"], p["c_kvnorm"], p["c_wn_t"],
                                                 p["c_wx1_t"], p["c_wx2_t"], p["c_wuk_bd_t"], hm)
    out_c = _mla_prompt(q_t, kcat, clat_t, p["c_wuv_t"], bsz, t_len)
    x = _merge(x, (out_a, out_b, out_c, out_d), u, p["w_branch"], p["w_o"], p["n_mix_post"])
    x = _ffn(x, p["n_ffn_pre"], p["w_gu"], p["w_down"], p["n_ffn_post"])
    c1 = jnp.stack([c_pad[:, h, 64 * (h % 2):64 * (h % 2) + 64, :64] for h in range(HEADS)], 1)
    n1 = jnp.stack([c_pad[:, h, 64 * (h % 2):64 * (h % 2) + 64, 64] for h in range(HEADS)], 1)
    m1 = m_pad[:, :HEADS, 0]
    state = (c1, n1, m1, pool1, conv1, ssm1.reshape(bsz, HEADS, DH, D_STATE))
    rows = (c_lat.reshape(bsz, t_len, C_KV_RANK), k_rope.reshape(bsz, t_len, C_ROPE))
    return x, state, rows


def _sample_layer(x, p, layer, tabs_s, state, page_table, cache_ckv, cache_kr, past_len):
    c0, n0, m0, pool0, conv0, ssm0 = state
    bsz = x.shape[0]
    u = _in_proj(x, p["n_mix_pre"], p["w_in"])
    sm = u[:, U_SMALL:U_SMALL + 128]
    out_a, c1, n1, m1 = _mlstm_step(u[:, 0:256], u[:, 256:512], u[:, 512:768], u[:, 768:1024],
                                    sm[:, SM_IG:SM_IG + 4], sm[:, SM_FG:SM_FG + 4], c0, n0, m0,
                                    p["a_bi"], p["a_bf"], p["a_norm"])
    pooled, pool1 = _pool_step(u[:, U_XB:U_XB + WIDTH], pool0, past_len)
    out_b = _pool_lin(pooled.reshape(bsz, WIDTH), p["b_wblk"], p["b_scale_row"])
    out_d, conv1, ssm1 = _ssd_step(u[:, U_DXBC:U_DXBC + D_CONV_DIM], u[:, U_DZ:U_DZ + WIDTH],
                                   sm[:, SM_DT:SM_DT + 4], conv0, ssm0, p["d_conv_w"], p["d_conv_b"],
                                   p["d_dt_bias"], p["d_a_log"], p["d_skip"], p["d_norm"])
    c_lat, k_rope, qa, qr = _mla_prep_sample(u, tabs_s[4], tabs_s[5], p["c_qnorm"], p["c_kvnorm"], p["c_wn"],
                                             p["c_wxa"], p["c_wxb"], p["c_wuk_bd"])
    lane_pad = 256 - C_KV_RANK - C_ROPE
    q8 = jnp.pad(jnp.concatenate([qa, qr], -1).transpose(1, 0, 2), ((0, 0), (0, 8 - HEADS), (0, lane_pad)))
    k_new = jnp.pad(jnp.concatenate([c_lat, k_rope], -1), ((0, 0), (0, lane_pad))).reshape(bsz, 1, 256)
    out_c = _mla_decode(page_table, q8, k_new, p["c_wuv"], cache_ckv, cache_kr, layer)
    outs = (out_a.reshape(bsz, WIDTH), out_b, out_c.reshape(bsz, WIDTH), out_d.reshape(bsz, WIDTH))
    x = _merge(x, outs, u, p["w_branch"], p["w_o"], p["n_mix_post"])
    x = _ffn(x, p["n_ffn_pre"], p["w_gu"], p["w_down"], p["n_ffn_post"])
    new_state = (c1, n1.reshape(bsz, HEADS, DH), m1.reshape(bsz, HEADS), pool1, conv1, ssm1)
    rows = (c_lat.reshape(bsz, 1, C_KV_RANK), k_rope.reshape(bsz, 1, C_ROPE))
    return x, new_state, rows


def kernel(x_prompt, x_sample, state_mlstm_c, state_mlstm_n, state_mlstm_m, state_pool, cache_ckv, cache_krope,
           page_table, state_conv, state_ssm, n_mix_pre, n_mix_post, n_ffn_pre, n_ffn_post, w_in, a_bi, a_bf,
           a_norm, b_w, b_scale, c_qnorm, c_wuq, c_kvnorm, c_wuk, c_wuv, d_conv_w, d_conv_b, d_dt_bias, d_a_log,
           d_skip, d_norm, w_branch, w_o, w_gu, w_down):
    bp, t_len, d = x_prompt.shape
    bs = x_sample.shape[0]
    assert x_sample.shape[1] == 1 and d == D_MODEL
    depth = w_in.shape[0]
    past_len = page_table.shape[1] * cache_ckv.shape[2]
    cache_krope = jnp.swapaxes(cache_krope, 2, 3)
    tabs_p = _rope_tables(jnp.arange(t_len))
    tabs_s = _rope_tables(jnp.full((1,), past_len))
    hm = _head_lane_mask()
    yp = x_prompt.reshape(bp * t_len, d)
    ys = x_sample.reshape(bs, d)
    st_p, st_s, rows_p, rows_s = [], [], [], []
    for l in range(depth):
        p = _layer_params(l, n_mix_pre, n_mix_post, n_ffn_pre, n_ffn_post, w_in, a_bi, a_bf, a_norm, b_w, b_scale,
                          c_qnorm, c_wuq, c_kvnorm, c_wuk, c_wuv, d_conv_w, d_conv_b, d_dt_bias, d_a_log, d_skip,
                          d_norm, w_branch, w_o, w_gu, w_down)
        yp, sp, rp = _prompt_layer(yp, bp, t_len, p, tabs_p, hm)
        in_state = (state_mlstm_c[l], state_mlstm_n[l], state_mlstm_m[l], state_pool[l], state_conv[l], state_ssm[l])
        ys, ss, rs = _sample_layer(ys, p, l, tabs_s, in_state, page_table, cache_ckv, cache_krope, past_len)
        st_p.append(sp)
        st_s.append(ss)
        rows_p.append(rp)
        rows_s.append(rs)

    stk = lambda lst, i: jnp.stack([s[i] for s in lst], 0)
    return (yp.reshape(bp, t_len, d), ys.reshape(bs, 1, d),
            stk(rows_p, 0), stk(rows_p, 1), stk(rows_s, 0), stk(rows_s, 1),
            stk(st_p, 0), stk(st_p, 1), stk(st_p, 2), stk(st_s, 0), stk(st_s, 1), stk(st_s, 2),
            stk(st_p, 3), stk(st_s, 3), stk(st_p, 4), stk(st_s, 4), stk(st_p, 5), stk(st_s, 5))
```

```python
import functools
import math

import jax
import jax.numpy as jnp
import numpy as np
from jax import lax
from jax.experimental import pallas as pl
from jax.experimental.pallas import tpu as pltpu

F32 = jnp.float32
BF16 = jnp.bfloat16
HIGHEST = lax.Precision.HIGHEST
NEG_INF = float("-inf")

EPS = 1e-6
D_MODEL = 1024
HEADS = 4
DH = 64
WIDTH = 256
C_ROPE = 32
C_KV_RANK = 128
ROPE_THETA = 10000.0
MLA_SCALE = (64 + C_ROPE) ** -0.5
D_STATE = 128
D_CONV = 4
D_CONV_DIM = 768
POOL_WINDOWS = (2, 4, 8, 16)
POOL_BUF = 15
FFN_HIDDEN = 2816
IN_SPLITS = (256, 256, 256, 256, 4, 4, 256, 256, 128, 32, 256, 768, 4, 4096)

U_MLSTM = 0
U_XB = 1024
U_CQ = 1280
U_DXBC = 1536
U_DZ = 2304
U_CKV = 2560
U_SMALL = 2688
U_K1 = 2816
U_K2 = 2944
U_GATES = 3072
U_WIDTH = 7168
SM_IG, SM_FG, SM_DT = 32, 40, 48

VMEM_LIMIT = 56 * 1024 * 1024
CHUNK = 256
PAGES_PER_STEP = 16
DECODE_SLOTS = 4


def _cparams(*sem):
    return pltpu.CompilerParams(dimension_semantics=sem, vmem_limit_bytes=VMEM_LIMIT)


def _rms(xf, g):
    return xf * lax.rsqrt(jnp.mean(xf * xf, -1, keepdims=True) + EPS) * g


def _softplus(x):
    return jnp.maximum(x, 0.0) + jnp.log1p(jnp.exp(-jnp.abs(x)))


def _log_sigmoid(x):
    return -_softplus(-x)


def _silu(x):
    return x * jax.nn.sigmoid(x)


def _dot(a, b):
    return jnp.dot(a, b, preferred_element_type=F32)


def _dot_nt(a, b):
    return lax.dot_general(a, b, (((1,), (1,)), ((), ())), preferred_element_type=F32)


def _dot_exact(a, b):
    return jnp.dot(a, b, precision=HIGHEST, preferred_element_type=F32)


def _tri(n):
    row = lax.broadcasted_iota(jnp.int32, (n, n), 0)
    col = lax.broadcasted_iota(jnp.int32, (n, n), 1)
    return row >= col, row <= col


def _in_proj_kernel(x_ref, g_ref, w_ref, o_ref, xn_ref):
    @pl.when(pl.program_id(1) == 0)
    def _():
        xn_ref[...] = _rms(x_ref[...], g_ref[...]).astype(BF16)

    o_ref[...] = _dot_nt(xn_ref[...], w_ref[...])


def _in_proj(x, g, w):
    n, d = x.shape
    wn = w.shape[0]
    tm = min(n, 1024)
    tn = 1792
    return pl.pallas_call(
        _in_proj_kernel,
        grid=(n // tm, wn // tn),
        in_specs=[pl.BlockSpec((tm, d), lambda i, j: (i, 0)),
                  pl.BlockSpec((1, d), lambda i, j: (0, 0)),
                  pl.BlockSpec((tn, d), lambda i, j: (j, 0))],
        out_specs=pl.BlockSpec((tm, tn), lambda i, j: (i, j)),
        out_shape=jax.ShapeDtypeStruct((n, wn), F32),
        scratch_shapes=[pltpu.VMEM((tm, d), BF16)],
        compiler_params=_cparams("parallel", "arbitrary"),
        name="in_proj",
    )(x, g, w)


def _cast_kernel(w_ref, o_ref):
    o_ref[...] = w_ref[...].astype(o_ref.dtype)


def _cast_bf16(w):
    n, k = w.shape
    tn = 1792 if n % 1792 == 0 else n
    return pl.pallas_call(
        _cast_kernel,
        grid=(n // tn,),
        in_specs=[pl.BlockSpec((tn, k), lambda j: (j, 0))],
        out_specs=pl.BlockSpec((tn, k), lambda j: (j, 0)),
        out_shape=jax.ShapeDtypeStruct((n, k), BF16),
        compiler_params=_cparams("parallel"),
        name="cast_bf16",
    )(w)


def _mlstm_kernel(qkvo_ref, sm_ref, bias_ref, norm_ref, out_ref, cst_ref, mst_ref, c_scr, m_scr):
    j = pl.program_id(1)
    L = qkvo_ref.shape[0]

    @pl.when(j == 0)
    def _():
        c_scr[...] = jnp.zeros_like(c_scr)
        m_scr[...] = jnp.zeros_like(m_scr)

    pre = sm_ref[...] + bias_ref[...]
    causal, anti = _tri(L)
    tril = causal.astype(F32)
    triu = anti.astype(F32)
    bcum = _dot_exact(tril, _log_sigmoid(pre))
    pre_t = pre.T
    ig_t = pre_t[SM_IG:SM_IG + 8, :]
    b_t = _dot_exact(_log_sigmoid(pre_t[SM_FG:SM_FG + 8, :]), triu)
    k_t = qkvo_ref[:, 256:512].T
    lane = lax.broadcasted_iota(jnp.int32, (L, 128), 1)
    sub = lax.broadcasted_iota(jnp.int32, (128, L), 0)

    scs, qcs, cps = [], [], []
    for h in range(HEADS):
        pair, half = divmod(h, 2)
        qq = qkvo_ref[:, 128 * pair:128 * pair + 128]
        kk = qkvo_ref[:, 256 + 128 * pair:256 + 128 * pair + 128]
        inhalf = (lane >= 64 * half) & (lane < 64 * half + 64)
        qm = jnp.where(inhalf, qq, 0.0).astype(BF16)
        cp = c_scr[h]
        cps.append(cp)
        scs.append(_dot_nt(qm, kk.astype(BF16)) * 0.125)
        qcs.append(_dot(qm, cp.astype(BF16)))

    hn_pairs = [jnp.zeros((L, 128), F32), jnp.zeros((L, 128), F32)]
    for h in range(HEADS):
        pair, half = divmod(h, 2)
        lo = 64 * half
        vv = qkvo_ref[:, 512 + 128 * pair:512 + 128 * pair + 128]
        bcol = bcum[:, SM_FG + h:SM_FG + h + 1]
        brow = b_t[h:h + 1, :]
        igrow = ig_t[h:h + 1, :]
        dlog = jnp.where(causal, bcol - brow + igrow, NEG_INF)
        m_h = m_scr[h:h + 1, 0:1]
        inter = bcol + m_h
        mt = jnp.maximum(jnp.max(dlog, -1, keepdims=True), inter)
        s = jnp.exp(dlog - mt) * scs[h]
        g = jnp.exp(inter - mt)
        vsh = vv if half == 0 else pltpu.roll(vv, 64, 1)
        vaug = jnp.where(lane < 64, vsh, jnp.where(lane == 64, 1.0, 0.0)).astype(BF16)
        nd = _dot(s.astype(BF16), vaug) + g * qcs[h]
        den = nd[:, 64:65]
        hh = nd / jnp.maximum(jnp.abs(den), jnp.exp(-mt))
        valid = lane < 64
        mu = jnp.sum(jnp.where(valid, hh, 0.0), -1, keepdims=True) * (1.0 / 64)
        dv = jnp.where(valid, hh - mu, 0.0)
        var = jnp.sum(dv * dv, -1, keepdims=True) * (1.0 / 64)
        hn = dv * lax.rsqrt(var + EPS)
        hn_pairs[pair] = hn_pairs[pair] + (hn if half == 0 else pltpu.roll(hn, 64, 1))
        bl = bcol[L - 1:L, :]
        wlog = bl - brow + igrow
        m_new = jnp.maximum(bl + m_h, jnp.max(wlog, -1, keepdims=True))
        w = jnp.exp(wlog - m_new)
        decay = jnp.exp(bl + m_h - m_new)
        inrows = (sub >= lo) & (sub < lo + 64)
        ktw = jnp.where(inrows, k_t[128 * pair:128 * pair + 128, :], 0.0) * (w * 0.125)
        c_scr[h] = decay * cps[h] + _dot(ktw.astype(BF16), vaug)
        m_scr[h:h + 1, :] = jnp.broadcast_to(m_new, (1, 128))
    for pair in range(2):
        og = qkvo_ref[:, 768 + 128 * pair:768 + 128 * pair + 128]
        out_ref[:, 128 * pair:128 * pair + 128] = (
            jax.nn.sigmoid(og) * (hn_pairs[pair] * norm_ref[:, 128 * pair:128 * pair + 128]))

    @pl.when(j == pl.num_programs(1) - 1)
    def _():
        cst_ref[0] = c_scr[...]
        mst_ref[0] = m_scr[...]


def _mlstm_prompt(u, bsz, t_len, bias_row, norm_row):
    L = min(CHUNK, t_len)
    nc = t_len // L
    n = bsz * t_len
    return pl.pallas_call(
        _mlstm_kernel,
        grid=(bsz, nc),
        in_specs=[pl.BlockSpec((L, 1024), lambda b, j: (b * nc + j, U_MLSTM // 1024)),
                  pl.BlockSpec((L, 128), lambda b, j: (b * nc + j, U_SMALL // 128)),
                  pl.BlockSpec((1, 128), lambda b, j: (0, 0)),
                  pl.BlockSpec((1, WIDTH), lambda b, j: (0, 0))],
        out_specs=[pl.BlockSpec((L, WIDTH), lambda b, j: (b * nc + j, 0)),
                   pl.BlockSpec((1, HEADS, 128, 128), lambda b, j: (b, 0, 0, 0)),
                   pl.BlockSpec((1, 8, 128), lambda b, j: (b, 0, 0))],
        out_shape=[jax.ShapeDtypeStruct((n, WIDTH), F32),
                   jax.ShapeDtypeStruct((bsz, HEADS, 128, 128), F32),
                   jax.ShapeDtypeStruct((bsz, 8, 128), F32)],
        scratch_shapes=[pltpu.VMEM((HEADS, 128, 128), F32), pltpu.VMEM((8, 128), F32)],
        compiler_params=_cparams("parallel", "arbitrary"),
        name="mlstm_prompt",
    )(u, u, bias_row, norm_row)


def _pool_kernel(xb_ref, w_ref, scale_ref, out_ref, st_ref, xs):
    j = pl.program_id(1)
    tt = xb_ref.shape[0]

    @pl.when(j == 0)
    def _():
        xs[0:16, :] = jnp.zeros((16, WIDTH), F32)

    xb = xb_ref[...]
    xs[16:16 + tt, :] = xb
    lane = lax.broadcasted_iota(jnp.int32, (tt, WIDTH), 1)
    pos = j * tt + lax.broadcasted_iota(jnp.int32, (tt, WIDTH), 0)
    acc = xb
    tot = jnp.zeros((tt, WIDTH), F32)
    cnt = jnp.zeros((tt, WIDTH), F32)
    for k in range(1, 16):
        acc = acc + xs[pl.ds(16 - k, tt), :]
        if k + 1 in POOL_WINDOWS:
            g = POOL_WINDOWS.index(k + 1)
            sel = (lane >= 64 * g) & (lane < 64 * g + 64)
            tot = jnp.where(sel, acc, tot)
            cnt = jnp.where(sel, jnp.minimum(pos + 1, k + 1).astype(F32), cnt)
    pooled = tot / cnt - xb
    out_ref[...] = _dot(pooled.astype(BF16), w_ref[...]) * scale_ref[...]
    xs[0:16, :] = xs[tt:tt + 16, :]

    @pl.when(j == pl.num_programs(1) - 1)
    def _():
        st_ref[0] = xs[1:16, :]


def _pool_prompt(u, bsz, t_len, w_blk, scale_row):
    tt = min(512, t_len)
    nt = t_len // tt
    n = bsz * t_len
    return pl.pallas_call(
        _pool_kernel,
        grid=(bsz, nt),
        in_specs=[pl.BlockSpec((tt, WIDTH), lambda b, j: (b * nt + j, U_XB // WIDTH)),
                  pl.BlockSpec((WIDTH, WIDTH), lambda b, j: (0, 0)),
                  pl.BlockSpec((1, WIDTH), lambda b, j: (0, 0))],
        out_specs=[pl.BlockSpec((tt, WIDTH), lambda b, j: (b * nt + j, 0)),
                   pl.BlockSpec((1, POOL_BUF, WIDTH), lambda b, j: (b, 0, 0))],
        out_shape=[jax.ShapeDtypeStruct((n, WIDTH), F32),
                   jax.ShapeDtypeStruct((bsz, POOL_BUF, WIDTH), F32)],
        scratch_shapes=[pltpu.VMEM((tt + 16, WIDTH), F32)],
        compiler_params=_cparams("parallel", "arbitrary"),
        name="pool_prompt",
    )(u, w_blk, scale_row)


def _ssd_kernel(dxbc_ref, dz_ref, sm_ref, cw_ref, cb_ref, bias_ref, alog_row_ref, alog_col_ref,
                skip_ref, norm_ref, out_ref, cst_ref, hst_ref, xs, hs):
    j = pl.program_id(1)
    L = dxbc_ref.shape[0]

    @pl.when(j == 0)
    def _():
        xs[0:8, :] = jnp.zeros((8, D_CONV_DIM), F32)
        hs[...] = jnp.zeros_like(hs)

    xs[8:8 + L, :] = dxbc_ref[...]
    conv = cb_ref[...] + xs[pl.ds(5, L), :] * cw_ref[0:1, :]
    for tap in range(1, D_CONV):
        conv = conv + xs[pl.ds(5 + tap, L), :] * cw_ref[tap:tap + 1, :]
    conv = _silu(conv)
    xs[0:8, :] = xs[L:L + 8, :]
    x4 = conv[:, 0:256]

    pre = sm_ref[...] + bias_ref[...]
    dtf = _softplus(pre)
    causal, anti = _tri(L)
    cum = _dot_exact(causal.astype(F32), dtf * (-jnp.exp(alog_row_ref[...])))
    dt_t = _softplus(pre.T[SM_DT:SM_DT + 8, :])
    cum_t = _dot_exact(dt_t * (-jnp.exp(alog_col_ref[...])), anti.astype(F32))

    lane = lax.broadcasted_iota(jnp.int32, (L, WIDTH), 1)
    dtb = jnp.zeros((L, WIDTH), F32)
    cumb = jnp.zeros((L, WIDTH), F32)
    for h in range(HEADS):
        sel = (lane >= 64 * h) & (lane < 64 * h + 64)
        dtb = jnp.where(sel, dtf[:, SM_DT + h:SM_DT + h + 1], dtb)
        cumb = jnp.where(sel, cum[:, SM_DT + h:SM_DT + h + 1], cumb)
    xdt = (x4 * dtb).astype(BF16)

    y = jnp.zeros((L, WIDTH), F32)
    y_inter = []
    for g in range(2):
        cg = conv[:, 512 + 128 * g:512 + 128 * g + 128].astype(BF16)
        bg = conv[:, 256 + 128 * g:256 + 128 * g + 128].astype(BF16)
        cb = _dot_nt(cg, bg)
        for hl in range(2):
            h = 2 * g + hl
            ccol = cum[:, SM_DT + h:SM_DT + h + 1]
            crow = cum_t[h:h + 1, :]
            seg = jnp.exp(jnp.where(causal, ccol - crow, NEG_INF))
            yh = _dot((cb * seg).astype(BF16), xdt)
            y = jnp.where((lane >= 64 * h) & (lane < 64 * h + 64), yh, y)
        y_inter.append(_dot_nt(cg, hs[128 * g:128 * g + 128, :].astype(BF16)))
    y = y + jnp.concatenate(y_inter, axis=1) * jnp.exp(cumb) + skip_ref[...] * x4
    y = y * _silu(dz_ref[...])
    for g in range(2):
        yg = y[:, 128 * g:128 * g + 128]
        yg = yg * lax.rsqrt(jnp.mean(yg * yg, -1, keepdims=True) + EPS)
        out_ref[:, 128 * g:128 * g + 128] = yg * norm_ref[:, 128 * g:128 * g + 128]

    x4_t = x4.T
    sub = lax.broadcasted_iota(jnp.int32, (128, L), 0)
    subc = lax.broadcasted_iota(jnp.int32, (128, 1), 0)
    for g in range(2):
        bg = conv[:, 256 + 128 * g:256 + 128 * g + 128].astype(BF16)
        rows, decs = [], []
        for hl in range(2):
            h = 2 * g + hl
            last = cum_t[h:h + 1, L - 1:L]
            rows.append(dt_t[h:h + 1, :] * jnp.exp(last - cum_t[h:h + 1, :]))
            decs.append(jnp.exp(last))
        x2t = x4_t[128 * g:128 * g + 128, :] * jnp.where(sub < 64, rows[0], rows[1])
        dec = jnp.where(subc < 64, decs[0], decs[1])
        hs[128 * g:128 * g + 128, :] = hs[128 * g:128 * g + 128, :] * dec + _dot(x2t.astype(BF16), bg)

    @pl.when(j == pl.num_programs(1) - 1)
    def _():
        cst_ref[0] = xs[5:8, :]
        hst_ref[0] = hs[...]


def _ssd_prompt(u, bsz, t_len, cw, cb, bias_row, alog_row, alog_col, skip_row, norm_row):
    L = min(CHUNK, t_len)
    nc = t_len // L
    n = bsz * t_len
    const = lambda shape: pl.BlockSpec(shape, lambda b, j: (0,) * len(shape))
    return pl.pallas_call(
        _ssd_kernel,
        grid=(bsz, nc),
        in_specs=[pl.BlockSpec((L, D_CONV_DIM), lambda b, j: (b * nc + j, U_DXBC // D_CONV_DIM)),
                  pl.BlockSpec((L, WIDTH), lambda b, j: (b * nc + j, U_DZ // WIDTH)),
                  pl.BlockSpec((L, 128), lambda b, j: (b * nc + j, U_SMALL // 128)),
                  const((D_CONV, D_CONV_DIM)), const((1, D_CONV_DIM)), const((1, 128)), const((1, 128)),
                  const((8, 1)), const((1, WIDTH)), const((1, WIDTH))],
        out_specs=[pl.BlockSpec((L, WIDTH), lambda b, j: (b * nc + j, 0)),
                   pl.BlockSpec((1, D_CONV - 1, D_CONV_DIM), lambda b, j: (b, 0, 0)),
                   pl.BlockSpec((1, HEADS * DH, D_STATE), lambda b, j: (b, 0, 0))],
        out_shape=[jax.ShapeDtypeStruct((n, WIDTH), F32),
                   jax.ShapeDtypeStruct((bsz, D_CONV - 1, D_CONV_DIM), F32),
                   jax.ShapeDtypeStruct((bsz, HEADS * DH, D_STATE), F32)],
        scratch_shapes=[pltpu.VMEM((L + 8, D_CONV_DIM), F32), pltpu.VMEM((HEADS * DH, D_STATE), F32)],
        compiler_params=_cparams("parallel", "arbitrary"),
        name="ssd_prompt",
    )(u, u, u, cw, cb, bias_row, alog_row, alog_col, skip_row, norm_row)


def _mla_prep_kernel(cq_ref, ckv_ref, k1_ref, k2_ref, cs1_ref, cs2_ref, cs1t_ref, cs2t_ref, csa_ref, csb_ref,
                     qn_ref, kvn_ref, wnt_ref, wx1t_ref, wx2t_ref, wukt_ref, hmt_ref,
                     clat_ref, kr_ref, kcat_ref, clatt_ref, qt_ref):
    ql_t = _rms(cq_ref[...], qn_ref[...]).T.astype(BF16)
    qnope_t = _dot(wnt_ref[...], ql_t)
    tail_t = _dot(wx1t_ref[...], ql_t) * cs1t_ref[...] + _dot(wx2t_ref[...], ql_t) * cs2t_ref[...]
    qabs_t = _dot(wukt_ref[...], qnope_t.astype(BF16))
    for h in range(HEADS):
        qt_ref[h, 0:128, :] = qabs_t[128 * h:128 * h + 128, :].astype(BF16)
        qt_ref[h, 128:256, :] = (tail_t * hmt_ref[h]).astype(BF16)
    clat = _rms(ckv_ref[...], kvn_ref[...])
    clat_ref[...] = clat
    clatt_ref[0] = clat.T.astype(BF16)
    k1 = k1_ref[...]
    k2 = k2_ref[...]
    kcat_ref[:, 0:128] = clat.astype(BF16)
    kcat_ref[:, 128:256] = (k1 * cs1_ref[...] + k2 * cs2_ref[...]).astype(BF16)
    kr_ref[...] = k1[:, 0:C_ROPE] * csa_ref[...] + k2[:, 0:C_ROPE] * csb_ref[...]


def _mla_prep(u, t_len, tabs, qn, kvn, wnt, wx1t, wx2t, wukt, hmt):
    n = u.shape[0]
    tm = min(CHUNK, t_len)
    ntab = t_len // tm
    tab = lambda w: pl.BlockSpec((tm, w), lambda i: (i % ntab, 0))
    tabt = pl.BlockSpec((128, tm), lambda i: (0, i % ntab))
    const = lambda shape: pl.BlockSpec(shape, lambda i: (0,) * len(shape))
    col = lambda w, off: pl.BlockSpec((tm, w), lambda i: (i, off // w))
    return pl.pallas_call(
        _mla_prep_kernel,
        grid=(n // tm,),
        in_specs=[col(256, U_CQ), col(128, U_CKV), col(128, U_K1), col(128, U_K2),
                  tab(128), tab(128), tabt, tabt, tab(C_ROPE), tab(C_ROPE),
                  const((1, 256)), const((1, 128)), const((256, 256)), const((128, 256)), const((128, 256)),
                  const((512, 256)), const((HEADS, 128, 1))],
        out_specs=[pl.BlockSpec((tm, 128), lambda i: (i, 0)),
                   pl.BlockSpec((tm, C_ROPE), lambda i: (i, 0)),
                   pl.BlockSpec((tm, 256), lambda i: (i, 0)),
                   pl.BlockSpec((1, 128, tm), lambda i: (i, 0, 0)),
                   pl.BlockSpec((HEADS, 256, tm), lambda i: (i, 0, 0))],
        out_shape=[jax.ShapeDtypeStruct((n, 128), F32),
                   jax.ShapeDtypeStruct((n, C_ROPE), F32),
                   jax.ShapeDtypeStruct((n, 256), BF16),
                   jax.ShapeDtypeStruct((n // tm, 128, tm), BF16),
                   jax.ShapeDtypeStruct((n // tm * HEADS, 256, tm), BF16)],
        compiler_params=_cparams("parallel"),
        name="mla_prep",
    )(u, u, u, u, *tabs, qn, kvn, wnt, wx1t, wx2t, wukt, hmt)


def _flash_kernel(qt_ref, k_ref, ct_ref, wuvt_ref, o_ref):
    qi = pl.program_id(1)
    tq = qt_ref.shape[2]
    key_idx = lax.broadcasted_iota(jnp.int32, (tq, tq), 0)
    qry_idx = lax.broadcasted_iota(jnp.int32, (tq, tq), 1)
    ahead = 3

    def scores(kb, h):
        k = k_ref[pl.ds(pl.multiple_of(kb * tq, tq), tq), :]
        return _dot(k, qt_ref[h]) * MLA_SCALE

    def step(kb, carry, last):
        state, s_in = carry
        c_t = ct_ref[kb]
        s_all = list(s_in)
        s_out = []
        out = []
        for h in range(HEADS):
            if h + ahead < HEADS:
                s_all.append(scores(kb, h + ahead))
            elif not last:
                s_out.append(scores(kb + 1, h + ahead - HEADS))
            m_prev, l_prev, acc = state[h]
            s = jnp.where(key_idx <= qry_idx, s_all[h], NEG_INF) if last else s_all[h]
            m_new = jnp.maximum(m_prev, jnp.max(s, axis=0, keepdims=True))
            alpha = jnp.exp(m_prev - m_new)
            p = jnp.exp(s - m_new)
            l_new = alpha * l_prev + jnp.sum(p, axis=0, keepdims=True)
            out.append((m_new, l_new, alpha * acc + _dot(c_t, p.astype(BF16))))
        return tuple(out), tuple(s_out)

    init = tuple((jnp.full((1, tq), NEG_INF, F32), jnp.zeros((1, tq), F32), jnp.zeros((128, tq), F32))
                 for _ in range(HEADS))
    carry = (init, tuple(scores(0, h) for h in range(ahead)))
    carry = lax.fori_loop(0, qi, functools.partial(step, last=False), carry)
    fin, _ = step(qi, carry, True)
    parts = []
    for h in range(HEADS):
        _, l_fin, acc_fin = fin[h]
        o_t = (acc_fin / l_fin).astype(BF16)
        parts.append(_dot(wuvt_ref[64 * h:64 * h + 64, :], o_t))
    o_ref[...] = jnp.concatenate(parts, axis=0).T


def _mla_prompt(q_t, kcat, clat_t, wuv_t, bsz, t_len):
    tq = min(CHUNK, t_len)
    nq = t_len // tq
    n = bsz * t_len
    return pl.pallas_call(
        _flash_kernel,
        grid=(bsz, nq),
        in_specs=[pl.BlockSpec((HEADS, 256, tq), lambda b, i: (b * nq + i, 0, 0)),
                  pl.BlockSpec((t_len, 256), lambda b, i: (b, 0)),
                  pl.BlockSpec((nq, 128, tq), lambda b, i: (b, 0, 0)),
                  pl.BlockSpec((WIDTH, 128), lambda b, i: (0, 0))],
        out_specs=pl.BlockSpec((tq, WIDTH), lambda b, i: (b * nq + i, 0)),
        out_shape=jax.ShapeDtypeStruct((n, WIDTH), F32),
        compiler_params=_cparams("parallel", "arbitrary"),
        name="mla_prompt",
    )(q_t, kcat, clat_t, wuv_t)


def _decode_kernel(pt_ref, q_ref, kn_ref, wuv_ref, ckv_hbm, kr_hbm, o_ref, *scratch, layer, n_pages, cp):
    bufc, bufr, sem = scratch[0:DECODE_SLOTS], scratch[DECODE_SLOTS:2 * DECODE_SLOTS], scratch[2 * DECODE_SLOTS]
    b = pl.program_id(0)
    nb = pl.num_programs(0)
    n_chunks = n_pages // cp
    n_total = nb * n_chunks

    def copies(g, slot, p):
        page = pt_ref[g * cp + p]
        return (pltpu.make_async_copy(ckv_hbm.at[layer, page], bufc[slot].at[p], sem.at[0, slot]),
                pltpu.make_async_copy(kr_hbm.at[layer, page], bufr[slot].at[p], sem.at[1, slot]))

    def start_chunk(g, slot):
        for p in range(cp):
            for cpy in copies(g, slot, p):
                cpy.start()

    def wait_chunk(g, slot):
        for p in range(cp):
            for cpy in copies(g, slot, p):
                cpy.wait()

    @pl.when(b == 0)
    def _():
        start_chunk(0, 0)
        start_chunk(1, 1)

    q = q_ref[0]
    q_lat = q[:, 0:C_KV_RANK]
    q_rope = q[:, C_KV_RANK:C_KV_RANK + C_ROPE]
    kn = kn_ref[0].astype(BF16).astype(F32)
    s_new = jnp.sum(q.astype(F32) * kn, -1, keepdims=True) * MLA_SCALE
    m0 = s_new
    l0 = jnp.ones((8, 1), F32)
    acc0 = jnp.broadcast_to(kn[:, 0:128], (8, 128))

    def scores(slot):
        kc = bufc[slot][...].reshape(cp * 128, 128).astype(BF16)
        s_rope = [_dot(q_rope, bufr[slot][p].astype(BF16)) for p in range(cp)]
        return kc, (_dot_nt(q_lat, kc) + jnp.concatenate(s_rope, axis=1)) * MLA_SCALE

    def update(carry, kc, s):
        m_prev, l_prev, acc = carry
        m_new = jnp.maximum(m_prev, jnp.max(s, -1, keepdims=True))
        alpha = jnp.exp(m_prev - m_new)
        p = jnp.exp(s - m_new)
        return m_new, alpha * l_prev + jnp.sum(p, -1, keepdims=True), alpha * acc + _dot(p.astype(BF16), kc)

    def pair(g, slot, carry):
        other = (slot + 2) % DECODE_SLOTS
        wait_chunk(g, slot)
        wait_chunk(g + 1, slot + 1)
        start_chunk(jnp.minimum(g + 2, n_total - 1), other)
        start_chunk(jnp.minimum(g + 3, n_total - 1), other + 1)
        kc_a, s_a = scores(slot)
        kc_b, s_b = scores(slot + 1)
        return update(update(carry, kc_a, s_a), kc_b, s_b)

    def body(it, carry):
        g = b * n_chunks + DECODE_SLOTS * it
        return pair(g + 2, 2, pair(g, 0, carry))

    _, l_fin, acc_fin = lax.fori_loop(0, n_chunks // DECODE_SLOTS, body, (m0, l0, acc0))

    @pl.when(b == nb - 1)
    def _():
        wait_chunk(n_total - 1, 0)
        wait_chunk(n_total - 1, 1)

    o = (acc_fin / l_fin).astype(BF16)
    of = _dot(o, wuv_ref[...])
    row = lax.broadcasted_iota(jnp.int32, (8, WIDTH), 0)
    lane = lax.broadcasted_iota(jnp.int32, (8, WIDTH), 1)
    diag = (lane >= 64 * row) & (lane < 64 * row + 64)
    o_ref[0] = jnp.sum(jnp.where(diag, of, 0.0), axis=0, keepdims=True)


def _mla_decode(page_table, q8, k_new, wuv, cache_ckv, cache_kr, layer):
    bsz, n_pages = page_table.shape
    cp = min(PAGES_PER_STEP, n_pages // DECODE_SLOTS)
    assert n_pages % (cp * DECODE_SLOTS) == 0
    page = cache_ckv.shape[2]
    kern = functools.partial(_decode_kernel, layer=layer, n_pages=n_pages, cp=cp)
    grid_spec = pltpu.PrefetchScalarGridSpec(
        num_scalar_prefetch=1,
        grid=(bsz,),
        in_specs=[pl.BlockSpec((1, 8, 256), lambda b, pt: (b, 0, 0)),
                  pl.BlockSpec((1, 1, 256), lambda b, pt: (b, 0, 0)),
                  pl.BlockSpec((128, WIDTH), lambda b, pt: (0, 0)),
                  pl.BlockSpec(memory_space=pl.ANY),
                  pl.BlockSpec(memory_space=pl.ANY)],
        out_specs=pl.BlockSpec((1, 1, WIDTH), lambda b, pt: (b, 0, 0)),
        scratch_shapes=[pltpu.VMEM((cp, page, 128), F32)] * DECODE_SLOTS
                       + [pltpu.VMEM((cp, C_ROPE, page), F32)] * DECODE_SLOTS
                       + [pltpu.SemaphoreType.DMA((2, DECODE_SLOTS))],
    )
    return pl.pallas_call(
        kern,
        grid_spec=grid_spec,
        out_shape=jax.ShapeDtypeStruct((bsz, 1, WIDTH), F32),
        compiler_params=_cparams("arbitrary"),
        name="mla_decode",
    )(page_table.reshape(-1), q8, k_new, wuv, cache_ckv, cache_kr)


def _mlstm_step_kernel(qr_ref, kr_ref, vr_ref, or_ref, qc_ref, kc_ref, ig_ref, fg_ref, c_ref, n_ref, m_ref,
                       bi_ref, bf_ref, norm_ref, out_ref, c1_ref, n1_ref, m1_ref):
    q = qr_ref[...]
    k = kr_ref[...] * 0.125
    ig = ig_ref[...] + bi_ref[...]
    lf = _log_sigmoid(fg_ref[...] + bf_ref[...])
    m0 = m_ref[...]
    c0 = c_ref[...]
    n0 = n_ref[...]
    inter = lf + m0
    mt = jnp.maximum(ig, inter)
    s = jnp.exp(ig - mt) * jnp.sum(q * k, -1, keepdims=True)
    g = jnp.exp(inter - mt)
    qc = jnp.sum(qc_ref[...] * c0, axis=2, keepdims=True)
    num = s * vr_ref[...] + g * qc
    den = s + g * jnp.sum(q * n0, -1, keepdims=True)
    hh = num / jnp.maximum(jnp.abs(den), jnp.exp(-mt))
    mu = jnp.mean(hh, -1, keepdims=True)
    dv = hh - mu
    var = jnp.mean(dv * dv, -1, keepdims=True)
    hn = dv * lax.rsqrt(var + EPS) * norm_ref[...]
    out_ref[...] = jax.nn.sigmoid(or_ref[...]) * hn
    w = jnp.exp(ig - mt)
    decay = jnp.exp(inter - mt)
    c1_ref[...] = decay * c0 + (w * 0.125 * kc_ref[...]) * vr_ref[...]
    n1_ref[...] = decay * n0 + w * k
    m1_ref[...] = mt


def _mlstm_step(q, k, v, o, ig, fg, c0, n0, m0, bi, bf, norm):
    bsz = q.shape[0]
    bb = min(16, bsz)
    r4 = lambda a: a.reshape(bsz, HEADS, 1, DH)
    c4 = lambda a: a.reshape(bsz, HEADS, DH, 1)
    s4 = lambda a: a.reshape(bsz, HEADS, 1, 1)
    blk = lambda d2, d3: pl.BlockSpec((bb, HEADS, d2, d3), lambda i: (i, 0, 0, 0))
    par = lambda d2, d3: pl.BlockSpec((1, HEADS, d2, d3), lambda i: (0, 0, 0, 0))
    return pl.pallas_call(
        _mlstm_step_kernel,
        grid=(bsz // bb,),
        in_specs=[blk(1, DH)] * 4 + [blk(DH, 1)] * 2 + [blk(1, 1)] * 2
                 + [blk(DH, DH), blk(1, DH), blk(1, 1), par(1, 1), par(1, 1), par(1, DH)],
        out_specs=[blk(1, DH), blk(DH, DH), blk(1, DH), blk(1, 1)],
        out_shape=[jax.ShapeDtypeStruct((bsz, HEADS, 1, DH), F32),
                   jax.ShapeDtypeStruct((bsz, HEADS, DH, DH), F32),
                   jax.ShapeDtypeStruct((bsz, HEADS, 1, DH), F32),
                   jax.ShapeDtypeStruct((bsz, HEADS, 1, 1), F32)],
        compiler_params=_cparams("parallel"),
        name="mlstm_step",
    )(r4(q), r4(k), r4(v), r4(o), c4(q), c4(k), s4(ig), s4(fg), c0, r4(n0), s4(m0),
      bi.reshape(1, HEADS, 1, 1), bf.reshape(1, HEADS, 1, 1), norm.reshape(1, HEADS, 1, DH))


def _pool_step_kernel(xb_ref, st_ref, pooled_ref, st1_ref, *, pos):
    xb = xb_ref[...]
    lane = lax.broadcasted_iota(jnp.int32, xb.shape, 2)
    acc = xb
    pooled = jnp.zeros_like(xb)
    for k in range(1, 16):
        acc = acc + st_ref[:, POOL_BUF - k:POOL_BUF - k + 1, :]
        if k + 1 in POOL_WINDOWS:
            g = POOL_WINDOWS.index(k + 1)
            cnt = float(min(pos + 1, k + 1))
            pooled = jnp.where((lane >= 64 * g) & (lane < 64 * g + 64), acc / cnt - xb, pooled)
    pooled_ref[...] = pooled
    st1_ref[:, 0:POOL_BUF - 1, :] = st_ref[:, 1:POOL_BUF, :]
    st1_ref[:, POOL_BUF - 1:POOL_BUF, :] = xb


def _pool_step(xb, state, pos):
    bsz = xb.shape[0]
    bb = min(32, bsz)
    return pl.pallas_call(
        functools.partial(_pool_step_kernel, pos=pos),
        grid=(bsz // bb,),
        in_specs=[pl.BlockSpec((bb, 1, WIDTH), lambda i: (i, 0, 0)),
                  pl.BlockSpec((bb, POOL_BUF, WIDTH), lambda i: (i, 0, 0))],
        out_specs=[pl.BlockSpec((bb, 1, WIDTH), lambda i: (i, 0, 0)),
                   pl.BlockSpec((bb, POOL_BUF, WIDTH), lambda i: (i, 0, 0))],
        out_shape=[jax.ShapeDtypeStruct((bsz, 1, WIDTH), F32),
                   jax.ShapeDtypeStruct((bsz, POOL_BUF, WIDTH), F32)],
        compiler_params=_cparams("parallel"),
        name="pool_step",
    )(xb.reshape(bsz, 1, WIDTH), state)


def _pool_lin_kernel(p_ref, w_ref, scale_ref, o_ref):
    o_ref[...] = _dot(p_ref[...].astype(BF16), w_ref[...]) * scale_ref[...]


def _pool_lin(pooled, w_blk, scale_row):
    n = pooled.shape[0]
    return pl.pallas_call(
        _pool_lin_kernel,
        out_shape=jax.ShapeDtypeStruct((n, WIDTH), F32),
        name="pool_lin",
    )(pooled, w_blk, scale_row)


def _ssd_step_kernel(x0_ref, x1_ref, x2_ref, x3_ref, bc_ref, st_ref, full_ref, dz_ref, ddt_ref, h_ref,
                     cwx_ref, cbx_ref, cwbc_ref, cbbc_ref, dtb_ref, alog_ref, skip_ref, norm_ref,
                     out_ref, st1_ref, h1_ref):
    xc = cbx_ref[...] + x0_ref[...] * cwx_ref[0] + x1_ref[...] * cwx_ref[1] \
        + x2_ref[...] * cwx_ref[2] + x3_ref[...] * cwx_ref[3]
    xc = _silu(xc)
    bc = cbbc_ref[...] + bc_ref[...] * cwbc_ref[3:4, :]
    for tap in range(D_CONV - 1):
        bc = bc + st_ref[:, tap:tap + 1, 256:768] * cwbc_ref[tap:tap + 1, :]
    bc = _silu(bc)
    dt = _softplus(ddt_ref[...] + dtb_ref[...])
    dec = jnp.exp(dt * (-jnp.exp(alog_ref[...])))
    xdt = xc * dt
    ys = []
    for g in range(2):
        bg = bc[:, :, 128 * g:128 * g + 128][:, None]
        cg = bc[:, :, 256 + 128 * g:256 + 128 * g + 128][:, None]
        h1 = h_ref[:, 2 * g:2 * g + 2] * dec[:, 2 * g:2 * g + 2] + xdt[:, 2 * g:2 * g + 2] * bg
        h1_ref[:, 2 * g:2 * g + 2] = h1
        ys.append(jnp.sum(h1 * cg, -1, keepdims=True))
    for g in range(2):
        y = ys[g] + skip_ref[:, 2 * g:2 * g + 2] * xc[:, 2 * g:2 * g + 2]
        y = y * _silu(dz_ref[:, 2 * g:2 * g + 2])
        ss = jnp.sum(y * y, axis=2, keepdims=True)
        ms = (ss[:, 0:1] + ss[:, 1:2]) * (1.0 / 128)
        out_ref[:, 2 * g:2 * g + 2] = y * lax.rsqrt(ms + EPS) * norm_ref[:, 2 * g:2 * g + 2]
    st1_ref[:, 0:D_CONV - 2, :] = st_ref[:, 1:D_CONV - 1, :]
    st1_ref[:, D_CONV - 2:D_CONV - 1, :] = full_ref[...]


def _ssd_step(dxbc, dz, ddt, state_conv, h0, cw, cb, dt_bias, a_log, skip, norm):
    bsz = dxbc.shape[0]
    bb = min(8, bsz)
    c4 = lambda a: a.reshape(bsz, HEADS, DH, 1)
    b4 = lambda d1, d2, d3: pl.BlockSpec((bb, d1, d2, d3), lambda i: (i, 0, 0, 0))
    b3 = lambda d1, d2: pl.BlockSpec((bb, d1, d2), lambda i: (i, 0, 0))
    par = lambda shape: pl.BlockSpec(shape, lambda i: (0,) * len(shape))
    col = b4(HEADS, DH, 1)
    return pl.pallas_call(
        _ssd_step_kernel,
        grid=(bsz // bb,),
        in_specs=[col, col, col, col, b3(1, 512), b3(D_CONV - 1, D_CONV_DIM), b3(1, D_CONV_DIM), col,
                  b4(HEADS, 1, 1), b4(HEADS, DH, D_STATE),
                  par((D_CONV, HEADS, DH, 1)), par((1, HEADS, DH, 1)), par((D_CONV, 512)), par((1, 512)),
                  par((1, HEADS, 1, 1)), par((1, HEADS, 1, 1)), par((1, HEADS, 1, 1)), par((1, HEADS, DH, 1))],
        out_specs=[col, b3(D_CONV - 1, D_CONV_DIM), b4(HEADS, DH, D_STATE)],
        out_shape=[jax.ShapeDtypeStruct((bsz, HEADS, DH, 1), F32),
                   jax.ShapeDtypeStruct((bsz, D_CONV - 1, D_CONV_DIM), F32),
                   jax.ShapeDtypeStruct((bsz, HEADS, DH, D_STATE), F32)],
        compiler_params=_cparams("parallel"),
        name="ssd_step",
    )(c4(state_conv[:, 0, :256]), c4(state_conv[:, 1, :256]), c4(state_conv[:, 2, :256]), c4(dxbc[:, :256]),
      dxbc[:, 256:].reshape(bsz, 1, 512), state_conv, dxbc.reshape(bsz, 1, D_CONV_DIM), c4(dz),
      ddt.reshape(bsz, HEADS, 1, 1), h0,
      cw[:, :256].reshape(D_CONV, HEADS, DH, 1), cb[:256].reshape(1, HEADS, DH, 1), cw[:, 256:],
      cb[256:].reshape(1, 512), dt_bias.reshape(1, HEADS, 1, 1), a_log.reshape(1, HEADS, 1, 1),
      skip.reshape(1, HEADS, 1, 1), norm.reshape(1, HEADS, DH, 1))


def _mla_prep_s_kernel(cq_ref, ckv_ref, k1_ref, k2_ref, csa_ref, csb_ref, qn_ref, kvn_ref, wn_ref, wxa_ref,
                       wxb_ref, wuk_ref, clat_ref, kr_ref, qa_ref, qr_ref):
    ql = _rms(cq_ref[...], qn_ref[...]).astype(BF16)
    qabs = _dot(_dot(ql, wn_ref[...]).astype(BF16), wuk_ref[...])
    csa = csa_ref[...]
    csb = csb_ref[...]
    for h in range(HEADS):
        qa_ref[h] = qabs[:, 128 * h:128 * h + 128].astype(BF16)
        qr_ref[h] = (_dot(ql, wxa_ref[h]) * csa + _dot(ql, wxb_ref[h]) * csb).astype(BF16)
    clat_ref[...] = _rms(ckv_ref[...], kvn_ref[...])
    kr_ref[...] = k1_ref[:, 0:C_ROPE] * csa + k2_ref[:, 0:C_ROPE] * csb


def _mla_prep_sample(u, csa, csb, qn, kvn, wn, wxa, wxb, wuk):
    n = u.shape[0]
    const = lambda shape: pl.BlockSpec(shape, lambda i: (0,) * len(shape))
    col = lambda w, off: pl.BlockSpec((n, w), lambda i: (0, off // w))
    return pl.pallas_call(
        _mla_prep_s_kernel,
        grid=(1,),
        in_specs=[col(256, U_CQ), col(128, U_CKV), col(128, U_K1), col(128, U_K2),
                  const((1, C_ROPE)), const((1, C_ROPE)), const((1, 256)), const((1, 128)), const((256, 256)),
                  const((HEADS, 256, C_ROPE)), const((HEADS, 256, C_ROPE)), const((256, 512))],
        out_specs=[const((n, 128)), const((n, C_ROPE)), const((HEADS, n, 128)), const((HEADS, n, C_ROPE))],
        out_shape=[jax.ShapeDtypeStruct((n, 128), F32),
                   jax.ShapeDtypeStruct((n, C_ROPE), F32),
                   jax.ShapeDtypeStruct((HEADS, n, 128), BF16),
                   jax.ShapeDtypeStruct((HEADS, n, C_ROPE), BF16)],
        compiler_params=_cparams("arbitrary"),
        name="mla_prep_sample",
    )(u, u, u, u, csa, csb, qn, kvn, wn, wxa, wxb, wuk)


def _merge_kernel(x_ref, oa_ref, ob_ref, oc_ref, od_ref, g0_ref, g1_ref, g2_ref, g3_ref, wb_ref, wo_ref,
                  gn_ref, out_ref):
    mixed = None
    for i, (o_ref, g_ref) in enumerate(((oa_ref, g0_ref), (ob_ref, g1_ref), (oc_ref, g2_ref), (od_ref, g3_ref))):
        term = jax.nn.sigmoid(g_ref[...]) * _dot(o_ref[...].astype(BF16), wb_ref[i])
        mixed = term if mixed is None else mixed + term
    y = _dot(mixed.astype(BF16), wo_ref[...])
    out_ref[...] = x_ref[...] + _rms(y, gn_ref[...])


def _merge(x, outs, u, wb, wo, gn):
    n, d = x.shape
    tm = min(512, n)
    row = lambda w: pl.BlockSpec((tm, w), lambda i: (i, 0))
    gate = lambda k: pl.BlockSpec((tm, d), lambda i: (i, U_GATES // d + k))
    return pl.pallas_call(
        _merge_kernel,
        grid=(n // tm,),
        in_specs=[row(d)] + [row(WIDTH)] * 4 + [gate(k) for k in range(4)]
                 + [pl.BlockSpec((4, WIDTH, d), lambda i: (0, 0, 0)), pl.BlockSpec((d, d), lambda i: (0, 0)),
                    pl.BlockSpec((1, d), lambda i: (0, 0))],
        out_specs=row(d),
        out_shape=jax.ShapeDtypeStruct((n, d), F32),
        compiler_params=_cparams("parallel"),
        name="merge",
    )(x, *outs, u, u, u, u, wb, wo, gn)


def _ffn_kernel(x_ref, g1_ref, wg_ref, wu_ref, wd_ref, g2_ref, out_ref, xn_scr, acc_scr):
    k = pl.program_id(1)

    @pl.when(k == 0)
    def _():
        xn_scr[...] = _rms(x_ref[...], g1_ref[...]).astype(BF16)
        acc_scr[...] = jnp.zeros_like(acc_scr)

    xn = xn_scr[...]
    hact = _silu(_dot(xn, wg_ref[...])) * _dot(xn, wu_ref[...])
    acc_scr[...] += _dot(hact.astype(BF16), wd_ref[...])

    @pl.when(k == pl.num_programs(1) - 1)
    def _():
        out_ref[...] = x_ref[...] + _rms(acc_scr[...], g2_ref[...])


def _ffn(x, g1, wgu, wd, g2):
    n, d = x.shape
    hid = wd.shape[0]
    tm = min(512, n)
    nk = 2
    th = hid // nk
    return pl.pallas_call(
        _ffn_kernel,
        grid=(n // tm, nk),
        in_specs=[pl.BlockSpec((tm, d), lambda i, k: (i, 0)),
                  pl.BlockSpec((1, d), lambda i, k: (0, 0)),
                  pl.BlockSpec((d, th), lambda i, k: (0, k)),
                  pl.BlockSpec((d, th), lambda i, k: (0, k + nk)),
                  pl.BlockSpec((th, d), lambda i, k: (k, 0)),
                  pl.BlockSpec((1, d), lambda i, k: (0, 0))],
        out_specs=pl.BlockSpec((tm, d), lambda i, k: (i, 0)),
        out_shape=jax.ShapeDtypeStruct((n, d), F32),
        scratch_shapes=[pltpu.VMEM((tm, d), BF16), pltpu.VMEM((tm, d), F32)],
        compiler_params=_cparams("parallel", "arbitrary"),
        name="ffn",
    )(x, g1, wgu, wgu, wd, g2)


def _lane_row(width, pieces):
    row = jnp.zeros((1, width), F32)
    for off, vals in pieces:
        row = row.at[0, off:off + vals.shape[0]].set(vals.astype(F32))
    return row


def _layer_params(l, n_mix_pre, n_mix_post, n_ffn_pre, n_ffn_post, w_in, a_bi, a_bf, a_norm, b_w, b_scale,
                  c_qnorm, c_wuq, c_kvnorm, c_wuk, c_wuv, d_conv_w, d_conv_b, d_dt_bias, d_a_log, d_skip, d_norm,
                  w_branch, w_o, w_gu, w_down):
    offs = np.concatenate([[0], np.cumsum(IN_SPLITS)]).tolist()
    w_l = w_in[:, l, :]
    seg = [w_l[offs[i]:offs[i + 1]] for i in range(len(IN_SPLITS))]
    aq, ak, av, ao, ai, af, xb, cq, ckv, ckr, dz, dxbc, ddt, gates = seg
    z = lambda w: jnp.zeros((w, D_MODEL), F32)
    small = jnp.concatenate([ckr, ai, z(4), af, z(4), ddt, z(128 - SM_DT - 4)], 0)
    k1 = jnp.tile(ckr[:16], (8, 1))
    k2 = jnp.tile(ckr[16:], (8, 1))
    p = {}
    p["w_in"] = _cast_bf16(jnp.concatenate([aq, ak, av, ao, xb, cq, dxbc, dz, ckv, small, k1, k2, gates], 0))
    p["n_mix_pre"] = n_mix_pre[l][None]
    p["n_mix_post"] = n_mix_post[l][None]
    p["n_ffn_pre"] = n_ffn_pre[l][None]
    p["n_ffn_post"] = n_ffn_post[l][None]
    p["a_bias_row"] = _lane_row(128, [(SM_IG, a_bi[l]), (SM_FG, a_bf[l])])
    p["a_norm_row"] = a_norm[l][None]
    p["a_bi"], p["a_bf"], p["a_norm"] = a_bi[l], a_bf[l], a_norm[l]
    wblk = jnp.zeros((WIDTH, WIDTH), F32)
    for g in range(4):
        wblk = wblk.at[64 * g:64 * g + 64, 64 * g:64 * g + 64].set(b_w[l][g])
    p["b_wblk"] = wblk.astype(BF16)
    p["b_scale_row"] = b_scale[l][None]
    wuq = c_wuq[l].reshape(256, HEADS, 96)
    p["c_wn"] = wuq[:, :, :64].reshape(256, 256).astype(BF16)
    p["c_wn_t"] = p["c_wn"].T
    wx1 = wuq[:, :, 64:80]
    wx2 = wuq[:, :, 80:96]
    p["c_wx1_t"] = jnp.tile(wx1.reshape(256, 64), (1, 2)).astype(BF16).T
    p["c_wx2_t"] = jnp.tile(wx2.reshape(256, 64), (1, 2)).astype(BF16).T
    p["c_wxa"] = jnp.concatenate([wx1, wx1], -1).transpose(1, 0, 2).astype(BF16)
    p["c_wxb"] = jnp.concatenate([wx2, wx2], -1).transpose(1, 0, 2).astype(BF16)
    wuk_bd = jnp.zeros((256, 512), F32)
    for h in range(HEADS):
        wuk_bd = wuk_bd.at[64 * h:64 * h + 64, 128 * h:128 * h + 128].set(c_wuk[l][:, h, :].T)
    p["c_wuk_bd"] = wuk_bd.astype(BF16)
    p["c_wuk_bd_t"] = p["c_wuk_bd"].T
    p["c_wuv"] = c_wuv[l].reshape(C_KV_RANK, WIDTH).astype(BF16)
    p["c_wuv_t"] = p["c_wuv"].T
    p["c_qnorm"] = c_qnorm[l][None]
    p["c_kvnorm"] = c_kvnorm[l][None]
    p["d_conv_w"] = d_conv_w[l]
    p["d_conv_b"] = d_conv_b[l]
    p["d_conv_b_row"] = d_conv_b[l][None]
    p["d_bias_row"] = _lane_row(128, [(SM_DT, d_dt_bias[l])])
    p["d_alog_row"] = _lane_row(128, [(SM_DT, d_a_log[l])])
    p["d_alog_col"] = _lane_row(8, [(0, d_a_log[l])]).reshape(8, 1)
    p["d_skip_row"] = jnp.repeat(d_skip[l], DH)[None]
    p["d_norm_row"] = d_norm[l][None]
    p["d_dt_bias"], p["d_a_log"], p["d_skip"], p["d_norm"] = d_dt_bias[l], d_a_log[l], d_skip[l], d_norm[l]
    p["w_branch"] = w_branch[l].astype(BF16)
    p["w_o"] = w_o[l].astype(BF16)
    p["w_gu"] = w_gu[l].astype(BF16)
    p["w_down"] = w_down[l].astype(BF16)
    return p


def _rope_tables(pos):
    inv = ROPE_THETA ** (-jnp.arange(16, dtype=F32) / 16)
    ang = pos.astype(F32)[:, None] * inv[None, :]
    cos, sin = jnp.cos(ang), jnp.sin(ang)
    cos64, sin64 = jnp.tile(cos, (1, 4)), jnp.tile(sin, (1, 4))
    cs1 = jnp.concatenate([cos64, sin64], 1)
    cs2 = jnp.concatenate([-sin64, cos64], 1)
    csa = jnp.concatenate([cos, sin], 1)
    csb = jnp.concatenate([-sin, cos], 1)
    return cs1, cs2, cs1.T, cs2.T, csa, csb


def _head_lane_mask():
    lane = np.arange(128)
    return jnp.asarray(((lane % 64) // 16)[None, :] == np.arange(HEADS)[:, None], F32)[:, :, None]


def _prompt_layer(x, bsz, t_len, p, tabs, hm):
    u = _in_proj(x, p["n_mix_pre"], p["w_in"])
    out_a, c_pad, m_pad = _mlstm_prompt(u, bsz, t_len, p["a_bias_row"], p["a_norm_row"])
    out_b, pool1 = _pool_prompt(u, bsz, t_len, p["b_wblk"], p["b_scale_row"])
    out_d, conv1, ssm1 = _ssd_prompt(u, bsz, t_len, p["d_conv_w"], p["d_conv_b_row"], p["d_bias_row"],
                                     p["d_alog_row"], p["d_alog_col"], p["d_skip_row"], p["d_norm_row"])
    c_lat, k_rope, kcat, clat_t, q_t = _mla_prep(u, t_len, tabs, p["c_qnorm"], p["c_kvnorm"], p["c_wn_t"],
                                                 p["c_wx1_t"], p["c_wx2_t"], p["c_wuk_bd_t"], hm)
    out_c = _mla_prompt(q_t, kcat, clat_t, p["c_wuv_t"], bsz, t_len)
    x = _merge(x, (out_a, out_b, out_c, out_d), u, p["w_branch"], p["w_o"], p["n_mix_post"])
    x = _ffn(x, p["n_ffn_pre"], p["w_gu"], p["w_down"], p["n_ffn_post"])
    c1 = jnp.stack([c_pad[:, h, 64 * (h % 2):64 * (h % 2) + 64, :64] for h in range(HEADS)], 1)
    n1 = jnp.stack([c_pad[:, h, 64 * (h % 2):64 * (h % 2) + 64, 64] for h in range(HEADS)], 1)
    m1 = m_pad[:, :HEADS, 0]
    state = (c1, n1, m1, pool1, conv1, ssm1.reshape(bsz, HEADS, DH, D_STATE))
    rows = (c_lat.reshape(bsz, t_len, C_KV_RANK), k_rope.reshape(bsz, t_len, C_ROPE))
    return x, state, rows


def _sample_layer(x, p, layer, tabs_s, state, page_table, cache_ckv, cache_kr, past_len):
    c0, n0, m0, pool0, conv0, ssm0 = state
    bsz = x.shape[0]
    u = _in_proj(x, p["n_mix_pre"], p["w_in"])
    sm = u[:, U_SMALL:U_SMALL + 128]
    out_a, c1, n1, m1 = _mlstm_step(u[:, 0:256], u[:, 256:512], u[:, 512:768], u[:, 768:1024],
                                    sm[:, SM_IG:SM_IG + 4], sm[:, SM_FG:SM_FG + 4], c0, n0, m0,
                                    p["a_bi"], p["a_bf"], p["a_norm"])
    pooled, pool1 = _pool_step(u[:, U_XB:U_XB + WIDTH], pool0, past_len)
    out_b = _pool_lin(pooled.reshape(bsz, WIDTH), p["b_wblk"], p["b_scale_row"])
    out_d, conv1, ssm1 = _ssd_step(u[:, U_DXBC:U_DXBC + D_CONV_DIM], u[:, U_DZ:U_DZ + WIDTH],
                                   sm[:, SM_DT:SM_DT + 4], conv0, ssm0, p["d_conv_w"], p["d_conv_b"],
                                   p["d_dt_bias"], p["d_a_log"], p["d_skip"], p["d_norm"])
    c_lat, k_rope, qa, qr = _mla_prep_sample(u, tabs_s[4], tabs_s[5], p["c_qnorm"], p["c_kvnorm"], p["c_wn"],
                                             p["c_wxa"], p["c_wxb"], p["c_wuk_bd"])
    lane_pad = 256 - C_KV_RANK - C_ROPE
    q8 = jnp.pad(jnp.concatenate([qa, qr], -1).transpose(1, 0, 2), ((0, 0), (0, 8 - HEADS), (0, lane_pad)))
    k_new = jnp.pad(jnp.concatenate([c_lat, k_rope], -1), ((0, 0), (0, lane_pad))).reshape(bsz, 1, 256)
    out_c = _mla_decode(page_table, q8, k_new, p["c_wuv"], cache_ckv, cache_kr, layer)
    outs = (out_a.reshape(bsz, WIDTH), out_b, out_c.reshape(bsz, WIDTH), out_d.reshape(bsz, WIDTH))
    x = _merge(x, outs, u, p["w_branch"], p["w_o"], p["n_mix_post"])
    x = _ffn(x, p["n_ffn_pre"], p["w_gu"], p["w_down"], p["n_ffn_post"])
    new_state = (c1, n1.reshape(bsz, HEADS, DH), m1.reshape(bsz, HEADS), pool1, conv1, ssm1)
    rows = (c_lat.reshape(bsz, 1, C_KV_RANK), k_rope.reshape(bsz, 1, C_ROPE))
    return x, new_state, rows


def kernel(x_prompt, x_sample, state_mlstm_c, state_mlstm_n, state_mlstm_m, state_pool, cache_ckv, cache_krope,
           page_table, state_conv, state_ssm, n_mix_pre, n_mix_post, n_ffn_pre, n_ffn_post, w_in, a_bi, a_bf,
           a_norm, b_w, b_scale, c_qnorm, c_wuq, c_kvnorm, c_wuk, c_wuv, d_conv_w, d_conv_b, d_dt_bias, d_a_log,
           d_skip, d_norm, w_branch, w_o, w_gu, w_down):
    bp, t_len, d = x_prompt.shape
    bs = x_sample.shape[0]
    assert x_sample.shape[1] == 1 and d == D_MODEL
    depth = w_in.shape[0]
    w_in = jnp.transpose(w_in, (2, 0, 1))
    past_len = page_table.shape[1] * cache_ckv.shape[2]
    cache_krope = jnp.swapaxes(cache_krope, 2, 3)
    tabs_p = _rope_tables(jnp.arange(t_len))
    tabs_s = _rope_tables(jnp.full((1,), past_len))
    hm = _head_lane_mask()
    yp = x_prompt.reshape(bp * t_len, d)
    ys = x_sample.reshape(bs, d)
    st_p, st_s, rows_p, rows_s = [], [], [], []
    for l in range(depth):
        p = _layer_params(l, n_mix_pre, n_mix_post, n_ffn_pre, n_ffn_post, w_in, a_bi, a_bf, a_norm, b_w, b_scale,
                          c_qnorm, c_wuq, c_kvnorm, c_wuk, c_wuv, d_conv_w, d_conv_b, d_dt_bias, d_a_log, d_skip,
                          d_norm, w_branch, w_o, w_gu, w_down)
        yp, sp, rp = _prompt_layer(yp, bp, t_len, p, tabs_p, hm)
        in_state = (state_mlstm_c[l], state_mlstm_n[l], state_mlstm_m[l], state_pool[l], state_conv[l], state_ssm[l])
        ys, ss, rs = _sample_layer(ys, p, l, tabs_s, in_state, page_table, cache_ckv, cache_krope, past_len)
        st_p.append(sp)
        st_s.append(ss)
        rows_p.append(rp)
        rows_s.append(rs)

    stk = lambda lst, i: jnp.stack([s[i] for s in lst], 0)
    return (yp.reshape(bp, t_len, d), ys.reshape(bs, 1, d),
            stk(rows_p, 0), stk(rows_p, 1), stk(rows_s, 0), stk(rows_s, 1),
            stk(st_p, 0), stk(st_p, 1), stk(st_p, 2), stk(st_s, 0), stk(st_s, 1), stk(st_s, 2),
            stk(st_p, 3), stk(st_s, 3), stk(st_p, 4), stk(st_s, 4), stk(st_p, 5), stk(st_s, 5))
```

```python
import functools
import math

import jax
import jax.numpy as jnp
import numpy as np
from jax import lax
from jax.experimental import pallas as pl
from jax.experimental.pallas import tpu as pltpu

F32 = jnp.float32
BF16 = jnp.bfloat16
HIGHEST = lax.Precision.HIGHEST
NEG_INF = float("-inf")

EPS = 1e-6
D_MODEL = 1024
HEADS = 4
DH = 64
WIDTH = 256
C_ROPE = 32
C_KV_RANK = 128
ROPE_THETA = 10000.0
MLA_SCALE = (64 + C_ROPE) ** -0.5
D_STATE = 128
D_CONV = 4
D_CONV_DIM = 768
POOL_WINDOWS = (2, 4, 8, 16)
POOL_BUF = 15
FFN_HIDDEN = 2816
IN_SPLITS = (256, 256, 256, 256, 4, 4, 256, 256, 128, 32, 256, 768, 4, 4096)

U_MLSTM = 0
U_XB = 1024
U_CQ = 1280
U_DXBC = 1536
U_DZ = 2304
U_CKV = 2560
U_SMALL = 2688
U_K1 = 2816
U_K2 = 2944
U_GATES = 3072
U_WIDTH = 7168
SM_IG, SM_FG, SM_DT = 32, 40, 48

VMEM_LIMIT = 56 * 1024 * 1024
CHUNK = 256
PAGES_PER_STEP = 32
DECODE_SLOTS = 4


def _cparams(*sem):
    return pltpu.CompilerParams(dimension_semantics=sem, vmem_limit_bytes=VMEM_LIMIT)


def _rms(xf, g):
    return xf * lax.rsqrt(jnp.mean(xf * xf, -1, keepdims=True) + EPS) * g


def _softplus(x):
    return jnp.maximum(x, 0.0) + jnp.log1p(jnp.exp(-jnp.abs(x)))


def _log_sigmoid(x):
    return -_softplus(-x)


def _silu(x):
    return x * jax.nn.sigmoid(x)


def _dot(a, b):
    return jnp.dot(a, b, preferred_element_type=F32)


def _dot_nt(a, b):
    return lax.dot_general(a, b, (((1,), (1,)), ((), ())), preferred_element_type=F32)


def _dot_exact(a, b):
    return jnp.dot(a, b, precision=HIGHEST, preferred_element_type=F32)


def _tri(n):
    row = lax.broadcasted_iota(jnp.int32, (n, n), 0)
    col = lax.broadcasted_iota(jnp.int32, (n, n), 1)
    return row >= col, row <= col


def _in_proj_kernel(x_ref, g_ref, w_ref, o_ref, xn_ref):
    @pl.when(pl.program_id(1) == 0)
    def _():
        xn_ref[...] = _rms(x_ref[...], g_ref[...]).astype(BF16)

    o_ref[...] = _dot_nt(xn_ref[...], w_ref[...])


def _in_proj(x, g, w):
    n, d = x.shape
    wn = w.shape[0]
    tm = min(n, 1024)
    tn = 1792
    return pl.pallas_call(
        _in_proj_kernel,
        grid=(n // tm, wn // tn),
        in_specs=[pl.BlockSpec((tm, d), lambda i, j: (i, 0)),
                  pl.BlockSpec((1, d), lambda i, j: (0, 0)),
                  pl.BlockSpec((tn, d), lambda i, j: (j, 0))],
        out_specs=pl.BlockSpec((tm, tn), lambda i, j: (i, j)),
        out_shape=jax.ShapeDtypeStruct((n, wn), F32),
        scratch_shapes=[pltpu.VMEM((tm, d), BF16)],
        compiler_params=_cparams("parallel", "arbitrary"),
        name="in_proj",
    )(x, g, w)


def _cast_kernel(w_ref, o_ref):
    o_ref[...] = w_ref[...].astype(o_ref.dtype)


def _cast_bf16(w):
    n, k = w.shape
    tn = 1792 if n % 1792 == 0 else n
    return pl.pallas_call(
        _cast_kernel,
        grid=(n // tn,),
        in_specs=[pl.BlockSpec((tn, k), lambda j: (j, 0))],
        out_specs=pl.BlockSpec((tn, k), lambda j: (j, 0)),
        out_shape=jax.ShapeDtypeStruct((n, k), BF16),
        compiler_params=_cparams("parallel"),
        name="cast_bf16",
    )(w)


def _mlstm_kernel(qkvo_ref, sm_ref, bias_ref, norm_ref, out_ref, cst_ref, mst_ref, c_scr, m_scr):
    j = pl.program_id(1)
    L = qkvo_ref.shape[0]

    @pl.when(j == 0)
    def _():
        c_scr[...] = jnp.zeros_like(c_scr)
        m_scr[...] = jnp.zeros_like(m_scr)

    pre = sm_ref[...] + bias_ref[...]
    causal, anti = _tri(L)
    tril = causal.astype(F32)
    triu = anti.astype(F32)
    bcum = _dot_exact(tril, _log_sigmoid(pre))
    pre_t = pre.T
    ig_t = pre_t[SM_IG:SM_IG + 8, :]
    b_t = _dot_exact(_log_sigmoid(pre_t[SM_FG:SM_FG + 8, :]), triu)
    k_t = qkvo_ref[:, 256:512].T
    lane = lax.broadcasted_iota(jnp.int32, (L, 128), 1)
    sub = lax.broadcasted_iota(jnp.int32, (128, L), 0)

    scs, qcs, cps = [], [], []
    for h in range(HEADS):
        pair, half = divmod(h, 2)
        qq = qkvo_ref[:, 128 * pair:128 * pair + 128]
        kk = qkvo_ref[:, 256 + 128 * pair:256 + 128 * pair + 128]
        inhalf = (lane >= 64 * half) & (lane < 64 * half + 64)
        qm = jnp.where(inhalf, qq, 0.0).astype(BF16)
        cp = c_scr[h]
        cps.append(cp)
        scs.append(_dot_nt(qm, kk.astype(BF16)) * 0.125)
        qcs.append(_dot(qm, cp.astype(BF16)))

    hn_pairs = [jnp.zeros((L, 128), F32), jnp.zeros((L, 128), F32)]
    for h in range(HEADS):
        pair, half = divmod(h, 2)
        lo = 64 * half
        vv = qkvo_ref[:, 512 + 128 * pair:512 + 128 * pair + 128]
        bcol = bcum[:, SM_FG + h:SM_FG + h + 1]
        brow = b_t[h:h + 1, :]
        igrow = ig_t[h:h + 1, :]
        dlog = jnp.where(causal, bcol - brow + igrow, NEG_INF)
        m_h = m_scr[h:h + 1, 0:1]
        inter = bcol + m_h
        mt = jnp.maximum(jnp.max(dlog, -1, keepdims=True), inter)
        s = jnp.exp(dlog - mt) * scs[h]
        g = jnp.exp(inter - mt)
        vsh = vv if half == 0 else pltpu.roll(vv, 64, 1)
        vaug = jnp.where(lane < 64, vsh, jnp.where(lane == 64, 1.0, 0.0)).astype(BF16)
        nd = _dot(s.astype(BF16), vaug) + g * qcs[h]
        den = nd[:, 64:65]
        hh = nd / jnp.maximum(jnp.abs(den), jnp.exp(-mt))
        valid = lane < 64
        mu = jnp.sum(jnp.where(valid, hh, 0.0), -1, keepdims=True) * (1.0 / 64)
        dv = jnp.where(valid, hh - mu, 0.0)
        var = jnp.sum(dv * dv, -1, keepdims=True) * (1.0 / 64)
        hn = dv * lax.rsqrt(var + EPS)
        hn_pairs[pair] = hn_pairs[pair] + (hn if half == 0 else pltpu.roll(hn, 64, 1))
        bl = bcol[L - 1:L, :]
        wlog = bl - brow + igrow
        m_new = jnp.maximum(bl + m_h, jnp.max(wlog, -1, keepdims=True))
        w = jnp.exp(wlog - m_new)
        decay = jnp.exp(bl + m_h - m_new)
        inrows = (sub >= lo) & (sub < lo + 64)
        ktw = jnp.where(inrows, k_t[128 * pair:128 * pair + 128, :], 0.0) * (w * 0.125)
        c_scr[h] = decay * cps[h] + _dot(ktw.astype(BF16), vaug)
        m_scr[h:h + 1, :] = jnp.broadcast_to(m_new, (1, 128))
    for pair in range(2):
        og = qkvo_ref[:, 768 + 128 * pair:768 + 128 * pair + 128]
        out_ref[:, 128 * pair:128 * pair + 128] = (
            jax.nn.sigmoid(og) * (hn_pairs[pair] * norm_ref[:, 128 * pair:128 * pair + 128]))

    @pl.when(j == pl.num_programs(1) - 1)
    def _():
        cst_ref[0] = c_scr[...]
        mst_ref[0] = m_scr[...]


def _mlstm_prompt(u, bsz, t_len, bias_row, norm_row):
    L = min(CHUNK, t_len)
    nc = t_len // L
    n = bsz * t_len
    return pl.pallas_call(
        _mlstm_kernel,
        grid=(bsz, nc),
        in_specs=[pl.BlockSpec((L, 1024), lambda b, j: (b * nc + j, U_MLSTM // 1024)),
                  pl.BlockSpec((L, 128), lambda b, j: (b * nc + j, U_SMALL // 128)),
                  pl.BlockSpec((1, 128), lambda b, j: (0, 0)),
                  pl.BlockSpec((1, WIDTH), lambda b, j: (0, 0))],
        out_specs=[pl.BlockSpec((L, WIDTH), lambda b, j: (b * nc + j, 0)),
                   pl.BlockSpec((1, HEADS, 128, 128), lambda b, j: (b, 0, 0, 0)),
                   pl.BlockSpec((1, 8, 128), lambda b, j: (b, 0, 0))],
        out_shape=[jax.ShapeDtypeStruct((n, WIDTH), F32),
                   jax.ShapeDtypeStruct((bsz, HEADS, 128, 128), F32),
                   jax.ShapeDtypeStruct((bsz, 8, 128), F32)],
        scratch_shapes=[pltpu.VMEM((HEADS, 128, 128), F32), pltpu.VMEM((8, 128), F32)],
        compiler_params=_cparams("parallel", "arbitrary"),
        name="mlstm_prompt",
    )(u, u, bias_row, norm_row)


def _pool_kernel(xb_ref, w_ref, scale_ref, out_ref, st_ref, xs):
    j = pl.program_id(1)
    tt = xb_ref.shape[0]

    @pl.when(j == 0)
    def _():
        xs[0:16, :] = jnp.zeros((16, WIDTH), F32)

    xb = xb_ref[...]
    xs[16:16 + tt, :] = xb
    lane = lax.broadcasted_iota(jnp.int32, (tt, WIDTH), 1)
    pos = j * tt + lax.broadcasted_iota(jnp.int32, (tt, WIDTH), 0)
    acc = xb
    tot = jnp.zeros((tt, WIDTH), F32)
    cnt = jnp.zeros((tt, WIDTH), F32)
    for k in range(1, 16):
        acc = acc + xs[pl.ds(16 - k, tt), :]
        if k + 1 in POOL_WINDOWS:
            g = POOL_WINDOWS.index(k + 1)
            sel = (lane >= 64 * g) & (lane < 64 * g + 64)
            tot = jnp.where(sel, acc, tot)
            cnt = jnp.where(sel, jnp.minimum(pos + 1, k + 1).astype(F32), cnt)
    pooled = tot / cnt - xb
    out_ref[...] = _dot(pooled.astype(BF16), w_ref[...]) * scale_ref[...]
    xs[0:16, :] = xs[tt:tt + 16, :]

    @pl.when(j == pl.num_programs(1) - 1)
    def _():
        st_ref[0] = xs[1:16, :]


def _pool_prompt(u, bsz, t_len, w_blk, scale_row):
    tt = min(512, t_len)
    nt = t_len // tt
    n = bsz * t_len
    return pl.pallas_call(
        _pool_kernel,
        grid=(bsz, nt),
        in_specs=[pl.BlockSpec((tt, WIDTH), lambda b, j: (b * nt + j, U_XB // WIDTH)),
                  pl.BlockSpec((WIDTH, WIDTH), lambda b, j: (0, 0)),
                  pl.BlockSpec((1, WIDTH), lambda b, j: (0, 0))],
        out_specs=[pl.BlockSpec((tt, WIDTH), lambda b, j: (b * nt + j, 0)),
                   pl.BlockSpec((1, POOL_BUF, WIDTH), lambda b, j: (b, 0, 0))],
        out_shape=[jax.ShapeDtypeStruct((n, WIDTH), F32),
                   jax.ShapeDtypeStruct((bsz, POOL_BUF, WIDTH), F32)],
        scratch_shapes=[pltpu.VMEM((tt + 16, WIDTH), F32)],
        compiler_params=_cparams("parallel", "arbitrary"),
        name="pool_prompt",
    )(u, w_blk, scale_row)


def _ssd_kernel(dxbc_ref, dz_ref, sm_ref, cw_ref, cb_ref, bias_ref, alog_row_ref, alog_col_ref,
                skip_ref, norm_ref, out_ref, cst_ref, hst_ref, xs, hs):
    j = pl.program_id(1)
    L = dxbc_ref.shape[0]

    @pl.when(j == 0)
    def _():
        xs[0:8, :] = jnp.zeros((8, D_CONV_DIM), F32)
        hs[...] = jnp.zeros_like(hs)

    xs[8:8 + L, :] = dxbc_ref[...]
    conv = cb_ref[...] + xs[pl.ds(5, L), :] * cw_ref[0:1, :]
    for tap in range(1, D_CONV):
        conv = conv + xs[pl.ds(5 + tap, L), :] * cw_ref[tap:tap + 1, :]
    conv = _silu(conv)
    xs[0:8, :] = xs[L:L + 8, :]
    x4 = conv[:, 0:256]

    pre = sm_ref[...] + bias_ref[...]
    dtf = _softplus(pre)
    causal, anti = _tri(L)
    cum = _dot_exact(causal.astype(F32), dtf * (-jnp.exp(alog_row_ref[...])))
    dt_t = _softplus(pre.T[SM_DT:SM_DT + 8, :])
    cum_t = _dot_exact(dt_t * (-jnp.exp(alog_col_ref[...])), anti.astype(F32))

    lane = lax.broadcasted_iota(jnp.int32, (L, WIDTH), 1)
    dtb = jnp.zeros((L, WIDTH), F32)
    cumb = jnp.zeros((L, WIDTH), F32)
    for h in range(HEADS):
        sel = (lane >= 64 * h) & (lane < 64 * h + 64)
        dtb = jnp.where(sel, dtf[:, SM_DT + h:SM_DT + h + 1], dtb)
        cumb = jnp.where(sel, cum[:, SM_DT + h:SM_DT + h + 1], cumb)
    xdt = (x4 * dtb).astype(BF16)

    y = jnp.zeros((L, WIDTH), F32)
    y_inter = []
    for g in range(2):
        cg = conv[:, 512 + 128 * g:512 + 128 * g + 128].astype(BF16)
        bg = conv[:, 256 + 128 * g:256 + 128 * g + 128].astype(BF16)
        cb = _dot_nt(cg, bg)
        for hl in range(2):
            h = 2 * g + hl
            ccol = cum[:, SM_DT + h:SM_DT + h + 1]
            crow = cum_t[h:h + 1, :]
            seg = jnp.exp(jnp.where(causal, ccol - crow, NEG_INF))
            yh = _dot((cb * seg).astype(BF16), xdt)
            y = jnp.where((lane >= 64 * h) & (lane < 64 * h + 64), yh, y)
        y_inter.append(_dot_nt(cg, hs[128 * g:128 * g + 128, :].astype(BF16)))
    y = y + jnp.concatenate(y_inter, axis=1) * jnp.exp(cumb) + skip_ref[...] * x4
    y = y * _silu(dz_ref[...])
    for g in range(2):
        yg = y[:, 128 * g:128 * g + 128]
        yg = yg * lax.rsqrt(jnp.mean(yg * yg, -1, keepdims=True) + EPS)
        out_ref[:, 128 * g:128 * g + 128] = yg * norm_ref[:, 128 * g:128 * g + 128]

    x4_t = x4.T
    sub = lax.broadcasted_iota(jnp.int32, (128, L), 0)
    subc = lax.broadcasted_iota(jnp.int32, (128, 1), 0)
    for g in range(2):
        bg = conv[:, 256 + 128 * g:256 + 128 * g + 128].astype(BF16)
        rows, decs = [], []
        for hl in range(2):
            h = 2 * g + hl
            last = cum_t[h:h + 1, L - 1:L]
            rows.append(dt_t[h:h + 1, :] * jnp.exp(last - cum_t[h:h + 1, :]))
            decs.append(jnp.exp(last))
        x2t = x4_t[128 * g:128 * g + 128, :] * jnp.where(sub < 64, rows[0], rows[1])
        dec = jnp.where(subc < 64, decs[0], decs[1])
        hs[128 * g:128 * g + 128, :] = hs[128 * g:128 * g + 128, :] * dec + _dot(x2t.astype(BF16), bg)

    @pl.when(j == pl.num_programs(1) - 1)
    def _():
        cst_ref[0] = xs[5:8, :]
        hst_ref[0] = hs[...]


def _ssd_prompt(u, bsz, t_len, cw, cb, bias_row, alog_row, alog_col, skip_row, norm_row):
    L = min(CHUNK, t_len)
    nc = t_len // L
    n = bsz * t_len
    const = lambda shape: pl.BlockSpec(shape, lambda b, j: (0,) * len(shape))
    return pl.pallas_call(
        _ssd_kernel,
        grid=(bsz, nc),
        in_specs=[pl.BlockSpec((L, D_CONV_DIM), lambda b, j: (b * nc + j, U_DXBC // D_CONV_DIM)),
                  pl.BlockSpec((L, WIDTH), lambda b, j: (b * nc + j, U_DZ // WIDTH)),
                  pl.BlockSpec((L, 128), lambda b, j: (b * nc + j, U_SMALL // 128)),
                  const((D_CONV, D_CONV_DIM)), const((1, D_CONV_DIM)), const((1, 128)), const((1, 128)),
                  const((8, 1)), const((1, WIDTH)), const((1, WIDTH))],
        out_specs=[pl.BlockSpec((L, WIDTH), lambda b, j: (b * nc + j, 0)),
                   pl.BlockSpec((1, D_CONV - 1, D_CONV_DIM), lambda b, j: (b, 0, 0)),
                   pl.BlockSpec((1, HEADS * DH, D_STATE), lambda b, j: (b, 0, 0))],
        out_shape=[jax.ShapeDtypeStruct((n, WIDTH), F32),
                   jax.ShapeDtypeStruct((bsz, D_CONV - 1, D_CONV_DIM), F32),
                   jax.ShapeDtypeStruct((bsz, HEADS * DH, D_STATE), F32)],
        scratch_shapes=[pltpu.VMEM((L + 8, D_CONV_DIM), F32), pltpu.VMEM((HEADS * DH, D_STATE), F32)],
        compiler_params=_cparams("parallel", "arbitrary"),
        name="ssd_prompt",
    )(u, u, u, cw, cb, bias_row, alog_row, alog_col, skip_row, norm_row)


def _mla_prep_kernel(cq_ref, ckv_ref, k1_ref, k2_ref, cs1_ref, cs2_ref, cs1t_ref, cs2t_ref, csa_ref, csb_ref,
                     qn_ref, kvn_ref, wnt_ref, wx1t_ref, wx2t_ref, wukt_ref, hmt_ref,
                     clat_ref, kr_ref, kcat_ref, clatt_ref, qt_ref):
    ql_t = _rms(cq_ref[...], qn_ref[...]).T.astype(BF16)
    qnope_t = _dot(wnt_ref[...], ql_t)
    tail_t = _dot(wx1t_ref[...], ql_t) * cs1t_ref[...] + _dot(wx2t_ref[...], ql_t) * cs2t_ref[...]
    qabs_t = _dot(wukt_ref[...], qnope_t.astype(BF16))
    for h in range(HEADS):
        qt_ref[h, 0:128, :] = qabs_t[128 * h:128 * h + 128, :].astype(BF16)
        qt_ref[h, 128:256, :] = (tail_t * hmt_ref[h]).astype(BF16)
    clat = _rms(ckv_ref[...], kvn_ref[...])
    clat_ref[...] = clat
    clatt_ref[0] = clat.T.astype(BF16)
    k1 = k1_ref[...]
    k2 = k2_ref[...]
    kcat_ref[:, 0:128] = clat.astype(BF16)
    kcat_ref[:, 128:256] = (k1 * cs1_ref[...] + k2 * cs2_ref[...]).astype(BF16)
    kr_ref[...] = k1[:, 0:C_ROPE] * csa_ref[...] + k2[:, 0:C_ROPE] * csb_ref[...]


def _mla_prep(u, t_len, tabs, qn, kvn, wnt, wx1t, wx2t, wukt, hmt):
    n = u.shape[0]
    tm = min(CHUNK, t_len)
    ntab = t_len // tm
    tab = lambda w: pl.BlockSpec((tm, w), lambda i: (i % ntab, 0))
    tabt = pl.BlockSpec((128, tm), lambda i: (0, i % ntab))
    const = lambda shape: pl.BlockSpec(shape, lambda i: (0,) * len(shape))
    col = lambda w, off: pl.BlockSpec((tm, w), lambda i: (i, off // w))
    return pl.pallas_call(
        _mla_prep_kernel,
        grid=(n // tm,),
        in_specs=[col(256, U_CQ), col(128, U_CKV), col(128, U_K1), col(128, U_K2),
                  tab(128), tab(128), tabt, tabt, tab(C_ROPE), tab(C_ROPE),
                  const((1, 256)), const((1, 128)), const((256, 256)), const((128, 256)), const((128, 256)),
                  const((512, 256)), const((HEADS, 128, 1))],
        out_specs=[pl.BlockSpec((tm, 128), lambda i: (i, 0)),
                   pl.BlockSpec((tm, C_ROPE), lambda i: (i, 0)),
                   pl.BlockSpec((tm, 256), lambda i: (i, 0)),
                   pl.BlockSpec((1, 128, tm), lambda i: (i, 0, 0)),
                   pl.BlockSpec((HEADS, 256, tm), lambda i: (i, 0, 0))],
        out_shape=[jax.ShapeDtypeStruct((n, 128), F32),
                   jax.ShapeDtypeStruct((n, C_ROPE), F32),
                   jax.ShapeDtypeStruct((n, 256), BF16),
                   jax.ShapeDtypeStruct((n // tm, 128, tm), BF16),
                   jax.ShapeDtypeStruct((n // tm * HEADS, 256, tm), BF16)],
        compiler_params=_cparams("parallel"),
        name="mla_prep",
    )(u, u, u, u, *tabs, qn, kvn, wnt, wx1t, wx2t, wukt, hmt)


def _flash_kernel(qt_ref, k_ref, ct_ref, wuvt_ref, o_ref):
    qi = pl.program_id(1)
    tq = qt_ref.shape[2]
    key_idx = lax.broadcasted_iota(jnp.int32, (tq, tq), 0)
    qry_idx = lax.broadcasted_iota(jnp.int32, (tq, tq), 1)
    ahead = 3

    def scores(kb, h):
        k = k_ref[pl.ds(pl.multiple_of(kb * tq, tq), tq), :]
        return _dot(k, qt_ref[h]) * MLA_SCALE

    def step(kb, carry, last):
        state, s_in = carry
        c_t = ct_ref[kb]
        s_all = list(s_in)
        s_out = []
        out = []
        for h in range(HEADS):
            if h + ahead < HEADS:
                s_all.append(scores(kb, h + ahead))
            elif not last:
                s_out.append(scores(kb + 1, h + ahead - HEADS))
            m_prev, l_prev, acc = state[h]
            s = jnp.where(key_idx <= qry_idx, s_all[h], NEG_INF) if last else s_all[h]
            m_new = jnp.maximum(m_prev, jnp.max(s, axis=0, keepdims=True))
            alpha = jnp.exp(m_prev - m_new)
            p = jnp.exp(s - m_new)
            l_new = alpha * l_prev + jnp.sum(p, axis=0, keepdims=True)
            out.append((m_new, l_new, alpha * acc + _dot(c_t, p.astype(BF16))))
        return tuple(out), tuple(s_out)

    init = tuple((jnp.full((1, tq), NEG_INF, F32), jnp.zeros((1, tq), F32), jnp.zeros((128, tq), F32))
                 for _ in range(HEADS))
    carry = (init, tuple(scores(0, h) for h in range(ahead)))
    carry = lax.fori_loop(0, qi, functools.partial(step, last=False), carry)
    fin, _ = step(qi, carry, True)
    parts = []
    for h in range(HEADS):
        _, l_fin, acc_fin = fin[h]
        o_t = (acc_fin / l_fin).astype(BF16)
        parts.append(_dot(wuvt_ref[64 * h:64 * h + 64, :], o_t))
    o_ref[...] = jnp.concatenate(parts, axis=0).T


def _mla_prompt(q_t, kcat, clat_t, wuv_t, bsz, t_len):
    tq = min(CHUNK, t_len)
    nq = t_len // tq
    n = bsz * t_len
    return pl.pallas_call(
        _flash_kernel,
        grid=(bsz, nq),
        in_specs=[pl.BlockSpec((HEADS, 256, tq), lambda b, i: (b * nq + i, 0, 0)),
                  pl.BlockSpec((t_len, 256), lambda b, i: (b, 0)),
                  pl.BlockSpec((nq, 128, tq), lambda b, i: (b, 0, 0)),
                  pl.BlockSpec((WIDTH, 128), lambda b, i: (0, 0))],
        out_specs=pl.BlockSpec((tq, WIDTH), lambda b, i: (b * nq + i, 0)),
        out_shape=jax.ShapeDtypeStruct((n, WIDTH), F32),
        compiler_params=_cparams("parallel", "arbitrary"),
        name="mla_prompt",
    )(q_t, kcat, clat_t, wuv_t)


def _decode_kernel(pt_ref, q_ref, kn_ref, wuv_ref, ckv_hbm, kr_hbm, o_ref, *scratch, layer, n_pages, cp):
    bufc, bufr, sem = scratch[0:DECODE_SLOTS], scratch[DECODE_SLOTS:2 * DECODE_SLOTS], scratch[2 * DECODE_SLOTS]
    b = pl.program_id(0)
    nb = pl.num_programs(0)
    n_chunks = n_pages // cp
    n_total = nb * n_chunks

    def copies(g, slot, p):
        page = pt_ref[g * cp + p]
        return (pltpu.make_async_copy(ckv_hbm.at[layer, page], bufc[slot].at[p], sem.at[0, slot]),
                pltpu.make_async_copy(kr_hbm.at[layer, page], bufr[slot].at[p], sem.at[1, slot]))

    def start_chunk(g, slot):
        for p in range(cp):
            for cpy in copies(g, slot, p):
                cpy.start()

    def wait_chunk(g, slot):
        for p in range(cp):
            for cpy in copies(g, slot, p):
                cpy.wait()

    @pl.when(b == 0)
    def _():
        start_chunk(0, 0)
        start_chunk(1, 1)

    q = q_ref[0]
    q_lat = q[:, 0:C_KV_RANK]
    q_rope = q[:, C_KV_RANK:C_KV_RANK + C_ROPE]
    kn = kn_ref[0].astype(BF16).astype(F32)
    s_new = jnp.sum(q.astype(F32) * kn, -1, keepdims=True) * MLA_SCALE
    m0 = s_new
    l0 = jnp.ones((8, 1), F32)
    acc0 = jnp.broadcast_to(kn[:, 0:128], (8, 128))

    def scores(slot):
        kc = bufc[slot][...].reshape(cp * 128, 128).astype(BF16)
        s_rope = [_dot(q_rope, bufr[slot][p].astype(BF16)) for p in range(cp)]
        return kc, (_dot_nt(q_lat, kc) + jnp.concatenate(s_rope, axis=1)) * MLA_SCALE

    def update(carry, kc, s):
        m_prev, l_prev, acc = carry
        m_new = jnp.maximum(m_prev, jnp.max(s, -1, keepdims=True))
        alpha = jnp.exp(m_prev - m_new)
        p = jnp.exp(s - m_new)
        return m_new, alpha * l_prev + jnp.sum(p, -1, keepdims=True), alpha * acc + _dot(p.astype(BF16), kc)

    def pair(g, slot, carry):
        other = (slot + 2) % DECODE_SLOTS
        wait_chunk(g, slot)
        wait_chunk(g + 1, slot + 1)
        start_chunk(jnp.minimum(g + 2, n_total - 1), other)
        start_chunk(jnp.minimum(g + 3, n_total - 1), other + 1)
        kc_a, s_a = scores(slot)
        kc_b, s_b = scores(slot + 1)
        return update(update(carry, kc_a, s_a), kc_b, s_b)

    def body(it, carry):
        g = b * n_chunks + DECODE_SLOTS * it
        return pair(g + 2, 2, pair(g, 0, carry))

    _, l_fin, acc_fin = lax.fori_loop(0, n_chunks // DECODE_SLOTS, body, (m0, l0, acc0))

    @pl.when(b == nb - 1)
    def _():
        wait_chunk(n_total - 1, 0)
        wait_chunk(n_total - 1, 1)

    o = (acc_fin / l_fin).astype(BF16)
    of = _dot(o, wuv_ref[...])
    row = lax.broadcasted_iota(jnp.int32, (8, WIDTH), 0)
    lane = lax.broadcasted_iota(jnp.int32, (8, WIDTH), 1)
    diag = (lane >= 64 * row) & (lane < 64 * row + 64)
    o_ref[0] = jnp.sum(jnp.where(diag, of, 0.0), axis=0, keepdims=True)


def _mla_decode(page_table, q8, k_new, wuv, cache_ckv, cache_kr, layer):
    bsz, n_pages = page_table.shape
    cp = min(PAGES_PER_STEP, n_pages // DECODE_SLOTS)
    assert n_pages % (cp * DECODE_SLOTS) == 0
    page = cache_ckv.shape[2]
    kern = functools.partial(_decode_kernel, layer=layer, n_pages=n_pages, cp=cp)
    grid_spec = pltpu.PrefetchScalarGridSpec(
        num_scalar_prefetch=1,
        grid=(bsz,),
        in_specs=[pl.BlockSpec((1, 8, 256), lambda b, pt: (b, 0, 0)),
                  pl.BlockSpec((1, 1, 256), lambda b, pt: (b, 0, 0)),
                  pl.BlockSpec((128, WIDTH), lambda b, pt: (0, 0)),
                  pl.BlockSpec(memory_space=pl.ANY),
                  pl.BlockSpec(memory_space=pl.ANY)],
        out_specs=pl.BlockSpec((1, 1, WIDTH), lambda b, pt: (b, 0, 0)),
        scratch_shapes=[pltpu.VMEM((cp, page, 128), F32)] * DECODE_SLOTS
                       + [pltpu.VMEM((cp, C_ROPE, page), F32)] * DECODE_SLOTS
                       + [pltpu.SemaphoreType.DMA((2, DECODE_SLOTS))],
    )
    return pl.pallas_call(
        kern,
        grid_spec=grid_spec,
        out_shape=jax.ShapeDtypeStruct((bsz, 1, WIDTH), F32),
        compiler_params=_cparams("arbitrary"),
        name="mla_decode",
    )(page_table.reshape(-1), q8, k_new, wuv, cache_ckv, cache_kr)


def _mlstm_step_kernel(qr_ref, kr_ref, vr_ref, or_ref, qc_ref, kc_ref, ig_ref, fg_ref, c_ref, n_ref, m_ref,
                       bi_ref, bf_ref, norm_ref, out_ref, c1_ref, n1_ref, m1_ref):
    q = qr_ref[...]
    k = kr_ref[...] * 0.125
    ig = ig_ref[...] + bi_ref[...]
    lf = _log_sigmoid(fg_ref[...] + bf_ref[...])
    m0 = m_ref[...]
    c0 = c_ref[...]
    n0 = n_ref[...]
    inter = lf + m0
    mt = jnp.maximum(ig, inter)
    s = jnp.exp(ig - mt) * jnp.sum(q * k, -1, keepdims=True)
    g = jnp.exp(inter - mt)
    qc = jnp.sum(qc_ref[...] * c0, axis=2, keepdims=True)
    num = s * vr_ref[...] + g * qc
    den = s + g * jnp.sum(q * n0, -1, keepdims=True)
    hh = num / jnp.maximum(jnp.abs(den), jnp.exp(-mt))
    mu = jnp.mean(hh, -1, keepdims=True)
    dv = hh - mu
    var = jnp.mean(dv * dv, -1, keepdims=True)
    hn = dv * lax.rsqrt(var + EPS) * norm_ref[...]
    out_ref[...] = jax.nn.sigmoid(or_ref[...]) * hn
    w = jnp.exp(ig - mt)
    decay = jnp.exp(inter - mt)
    c1_ref[...] = decay * c0 + (w * 0.125 * kc_ref[...]) * vr_ref[...]
    n1_ref[...] = decay * n0 + w * k
    m1_ref[...] = mt


def _mlstm_step(q, k, v, o, ig, fg, c0, n0, m0, bi, bf, norm):
    bsz = q.shape[0]
    bb = min(16, bsz)
    r4 = lambda a: a.reshape(bsz, HEADS, 1, DH)
    c4 = lambda a: a.reshape(bsz, HEADS, DH, 1)
    s4 = lambda a: a.reshape(bsz, HEADS, 1, 1)
    blk = lambda d2, d3: pl.BlockSpec((bb, HEADS, d2, d3), lambda i: (i, 0, 0, 0))
    par = lambda d2, d3: pl.BlockSpec((1, HEADS, d2, d3), lambda i: (0, 0, 0, 0))
    return pl.pallas_call(
        _mlstm_step_kernel,
        grid=(bsz // bb,),
        in_specs=[blk(1, DH)] * 4 + [blk(DH, 1)] * 2 + [blk(1, 1)] * 2
                 + [blk(DH, DH), blk(1, DH), blk(1, 1), par(1, 1), par(1, 1), par(1, DH)],
        out_specs=[blk(1, DH), blk(DH, DH), blk(1, DH), blk(1, 1)],
        out_shape=[jax.ShapeDtypeStruct((bsz, HEADS, 1, DH), F32),
                   jax.ShapeDtypeStruct((bsz, HEADS, DH, DH), F32),
                   jax.ShapeDtypeStruct((bsz, HEADS, 1, DH), F32),
                   jax.ShapeDtypeStruct((bsz, HEADS, 1, 1), F32)],
        compiler_params=_cparams("parallel"),
        name="mlstm_step",
    )(r4(q), r4(k), r4(v), r4(o), c4(q), c4(k), s4(ig), s4(fg), c0, r4(n0), s4(m0),
      bi.reshape(1, HEADS, 1, 1), bf.reshape(1, HEADS, 1, 1), norm.reshape(1, HEADS, 1, DH))


def _pool_step_kernel(xb_ref, st_ref, pooled_ref, st1_ref, *, pos):
    xb = xb_ref[...]
    lane = lax.broadcasted_iota(jnp.int32, xb.shape, 2)
    acc = xb
    pooled = jnp.zeros_like(xb)
    for k in range(1, 16):
        acc = acc + st_ref[:, POOL_BUF - k:POOL_BUF - k + 1, :]
        if k + 1 in POOL_WINDOWS:
            g = POOL_WINDOWS.index(k + 1)
            cnt = float(min(pos + 1, k + 1))
            pooled = jnp.where((lane >= 64 * g) & (lane < 64 * g + 64), acc / cnt - xb, pooled)
    pooled_ref[...] = pooled
    st1_ref[:, 0:POOL_BUF - 1, :] = st_ref[:, 1:POOL_BUF, :]
    st1_ref[:, POOL_BUF - 1:POOL_BUF, :] = xb


def _pool_step(xb, state, pos):
    bsz = xb.shape[0]
    bb = min(32, bsz)
    return pl.pallas_call(
        functools.partial(_pool_step_kernel, pos=pos),
        grid=(bsz // bb,),
        in_specs=[pl.BlockSpec((bb, 1, WIDTH), lambda i: (i, 0, 0)),
                  pl.BlockSpec((bb, POOL_BUF, WIDTH), lambda i: (i, 0, 0))],
        out_specs=[pl.BlockSpec((bb, 1, WIDTH), lambda i: (i, 0, 0)),
                   pl.BlockSpec((bb, POOL_BUF, WIDTH), lambda i: (i, 0, 0))],
        out_shape=[jax.ShapeDtypeStruct((bsz, 1, WIDTH), F32),
                   jax.ShapeDtypeStruct((bsz, POOL_BUF, WIDTH), F32)],
        compiler_params=_cparams("parallel"),
        name="pool_step",
    )(xb.reshape(bsz, 1, WIDTH), state)


def _pool_lin_kernel(p_ref, w_ref, scale_ref, o_ref):
    o_ref[...] = _dot(p_ref[...].astype(BF16), w_ref[...]) * scale_ref[...]


def _pool_lin(pooled, w_blk, scale_row):
    n = pooled.shape[0]
    return pl.pallas_call(
        _pool_lin_kernel,
        out_shape=jax.ShapeDtypeStruct((n, WIDTH), F32),
        name="pool_lin",
    )(pooled, w_blk, scale_row)


def _ssd_step_kernel(x0_ref, x1_ref, x2_ref, x3_ref, bc_ref, st_ref, full_ref, dz_ref, ddt_ref, h_ref,
                     cwx_ref, cbx_ref, cwbc_ref, cbbc_ref, dtb_ref, alog_ref, skip_ref, norm_ref,
                     out_ref, st1_ref, h1_ref):
    xc = cbx_ref[...] + x0_ref[...] * cwx_ref[0] + x1_ref[...] * cwx_ref[1] \
        + x2_ref[...] * cwx_ref[2] + x3_ref[...] * cwx_ref[3]
    xc = _silu(xc)
    bc = cbbc_ref[...] + bc_ref[...] * cwbc_ref[3:4, :]
    for tap in range(D_CONV - 1):
        bc = bc + st_ref[:, tap:tap + 1, 256:768] * cwbc_ref[tap:tap + 1, :]
    bc = _silu(bc)
    dt = _softplus(ddt_ref[...] + dtb_ref[...])
    dec = jnp.exp(dt * (-jnp.exp(alog_ref[...])))
    xdt = xc * dt
    ys = []
    for g in range(2):
        bg = bc[:, :, 128 * g:128 * g + 128][:, None]
        cg = bc[:, :, 256 + 128 * g:256 + 128 * g + 128][:, None]
        h1 = h_ref[:, 2 * g:2 * g + 2] * dec[:, 2 * g:2 * g + 2] + xdt[:, 2 * g:2 * g + 2] * bg
        h1_ref[:, 2 * g:2 * g + 2] = h1
        ys.append(jnp.sum(h1 * cg, -1, keepdims=True))
    for g in range(2):
        y = ys[g] + skip_ref[:, 2 * g:2 * g + 2] * xc[:, 2 * g:2 * g + 2]
        y = y * _silu(dz_ref[:, 2 * g:2 * g + 2])
        ss = jnp.sum(y * y, axis=2, keepdims=True)
        ms = (ss[:, 0:1] + ss[:, 1:2]) * (1.0 / 128)
        out_ref[:, 2 * g:2 * g + 2] = y * lax.rsqrt(ms + EPS) * norm_ref[:, 2 * g:2 * g + 2]
    st1_ref[:, 0:D_CONV - 2, :] = st_ref[:, 1:D_CONV - 1, :]
    st1_ref[:, D_CONV - 2:D_CONV - 1, :] = full_ref[...]


def _ssd_step(dxbc, dz, ddt, state_conv, h0, cw, cb, dt_bias, a_log, skip, norm):
    bsz = dxbc.shape[0]
    bb = min(8, bsz)
    c4 = lambda a: a.reshape(bsz, HEADS, DH, 1)
    b4 = lambda d1, d2, d3: pl.BlockSpec((bb, d1, d2, d3), lambda i: (i, 0, 0, 0))
    b3 = lambda d1, d2: pl.BlockSpec((bb, d1, d2), lambda i: (i, 0, 0))
    par = lambda shape: pl.BlockSpec(shape, lambda i: (0,) * len(shape))
    col = b4(HEADS, DH, 1)
    return pl.pallas_call(
        _ssd_step_kernel,
        grid=(bsz // bb,),
        in_specs=[col, col, col, col, b3(1, 512), b3(D_CONV - 1, D_CONV_DIM), b3(1, D_CONV_DIM), col,
                  b4(HEADS, 1, 1), b4(HEADS, DH, D_STATE),
                  par((D_CONV, HEADS, DH, 1)), par((1, HEADS, DH, 1)), par((D_CONV, 512)), par((1, 512)),
                  par((1, HEADS, 1, 1)), par((1, HEADS, 1, 1)), par((1, HEADS, 1, 1)), par((1, HEADS, DH, 1))],
        out_specs=[col, b3(D_CONV - 1, D_CONV_DIM), b4(HEADS, DH, D_STATE)],
        out_shape=[jax.ShapeDtypeStruct((bsz, HEADS, DH, 1), F32),
                   jax.ShapeDtypeStruct((bsz, D_CONV - 1, D_CONV_DIM), F32),
                   jax.ShapeDtypeStruct((bsz, HEADS, DH, D_STATE), F32)],
        compiler_params=_cparams("parallel"),
        name="ssd_step",
    )(c4(state_conv[:, 0, :256]), c4(state_conv[:, 1, :256]), c4(state_conv[:, 2, :256]), c4(dxbc[:, :256]),
      dxbc[:, 256:].reshape(bsz, 1, 512), state_conv, dxbc.reshape(bsz, 1, D_CONV_DIM), c4(dz),
      ddt.reshape(bsz, HEADS, 1, 1), h0,
      cw[:, :256].reshape(D_CONV, HEADS, DH, 1), cb[:256].reshape(1, HEADS, DH, 1), cw[:, 256:],
      cb[256:].reshape(1, 512), dt_bias.reshape(1, HEADS, 1, 1), a_log.reshape(1, HEADS, 1, 1),
      skip.reshape(1, HEADS, 1, 1), norm.reshape(1, HEADS, DH, 1))


def _mla_prep_s_kernel(cq_ref, ckv_ref, k1_ref, k2_ref, csa_ref, csb_ref, qn_ref, kvn_ref, wn_ref, wxa_ref,
                       wxb_ref, wuk_ref, clat_ref, kr_ref, qa_ref, qr_ref):
    ql = _rms(cq_ref[...], qn_ref[...]).astype(BF16)
    qabs = _dot(_dot(ql, wn_ref[...]).astype(BF16), wuk_ref[...])
    csa = csa_ref[...]
    csb = csb_ref[...]
    for h in range(HEADS):
        qa_ref[h] = qabs[:, 128 * h:128 * h + 128].astype(BF16)
        qr_ref[h] = (_dot(ql, wxa_ref[h]) * csa + _dot(ql, wxb_ref[h]) * csb).astype(BF16)
    clat_ref[...] = _rms(ckv_ref[...], kvn_ref[...])
    kr_ref[...] = k1_ref[:, 0:C_ROPE] * csa + k2_ref[:, 0:C_ROPE] * csb


def _mla_prep_sample(u, csa, csb, qn, kvn, wn, wxa, wxb, wuk):
    n = u.shape[0]
    const = lambda shape: pl.BlockSpec(shape, lambda i: (0,) * len(shape))
    col = lambda w, off: pl.BlockSpec((n, w), lambda i: (0, off // w))
    return pl.pallas_call(
        _mla_prep_s_kernel,
        grid=(1,),
        in_specs=[col(256, U_CQ), col(128, U_CKV), col(128, U_K1), col(128, U_K2),
                  const((1, C_ROPE)), const((1, C_ROPE)), const((1, 256)), const((1, 128)), const((256, 256)),
                  const((HEADS, 256, C_ROPE)), const((HEADS, 256, C_ROPE)), const((256, 512))],
        out_specs=[const((n, 128)), const((n, C_ROPE)), const((HEADS, n, 128)), const((HEADS, n, C_ROPE))],
        out_shape=[jax.ShapeDtypeStruct((n, 128), F32),
                   jax.ShapeDtypeStruct((n, C_ROPE), F32),
                   jax.ShapeDtypeStruct((HEADS, n, 128), BF16),
                   jax.ShapeDtypeStruct((HEADS, n, C_ROPE), BF16)],
        compiler_params=_cparams("arbitrary"),
        name="mla_prep_sample",
    )(u, u, u, u, csa, csb, qn, kvn, wn, wxa, wxb, wuk)


def _merge_kernel(x_ref, oa_ref, ob_ref, oc_ref, od_ref, g0_ref, g1_ref, g2_ref, g3_ref, wb_ref, wo_ref,
                  gn_ref, out_ref):
    mixed = None
    for i, (o_ref, g_ref) in enumerate(((oa_ref, g0_ref), (ob_ref, g1_ref), (oc_ref, g2_ref), (od_ref, g3_ref))):
        term = jax.nn.sigmoid(g_ref[...]) * _dot(o_ref[...].astype(BF16), wb_ref[i])
        mixed = term if mixed is None else mixed + term
    y = _dot(mixed.astype(BF16), wo_ref[...])
    out_ref[...] = x_ref[...] + _rms(y, gn_ref[...])


def _merge(x, outs, u, wb, wo, gn):
    n, d = x.shape
    tm = min(512, n)
    row = lambda w: pl.BlockSpec((tm, w), lambda i: (i, 0))
    gate = lambda k: pl.BlockSpec((tm, d), lambda i: (i, U_GATES // d + k))
    return pl.pallas_call(
        _merge_kernel,
        grid=(n // tm,),
        in_specs=[row(d)] + [row(WIDTH)] * 4 + [gate(k) for k in range(4)]
                 + [pl.BlockSpec((4, WIDTH, d), lambda i: (0, 0, 0)), pl.BlockSpec((d, d), lambda i: (0, 0)),
                    pl.BlockSpec((1, d), lambda i: (0, 0))],
        out_specs=row(d),
        out_shape=jax.ShapeDtypeStruct((n, d), F32),
        compiler_params=_cparams("parallel"),
        name="merge",
    )(x, *outs, u, u, u, u, wb, wo, gn)


def _ffn_kernel(x_ref, g1_ref, wg_ref, wu_ref, wd_ref, g2_ref, out_ref, xn_scr, acc_scr):
    k = pl.program_id(1)

    @pl.when(k == 0)
    def _():
        xn_scr[...] = _rms(x_ref[...], g1_ref[...]).astype(BF16)
        acc_scr[...] = jnp.zeros_like(acc_scr)

    xn = xn_scr[...]
    hact = _silu(_dot(xn, wg_ref[...])) * _dot(xn, wu_ref[...])
    acc_scr[...] += _dot(hact.astype(BF16), wd_ref[...])

    @pl.when(k == pl.num_programs(1) - 1)
    def _():
        out_ref[...] = x_ref[...] + _rms(acc_scr[...], g2_ref[...])


def _ffn(x, g1, wgu, wd, g2):
    n, d = x.shape
    hid = wd.shape[0]
    tm = min(512, n)
    nk = 2
    th = hid // nk
    return pl.pallas_call(
        _ffn_kernel,
        grid=(n // tm, nk),
        in_specs=[pl.BlockSpec((tm, d), lambda i, k: (i, 0)),
                  pl.BlockSpec((1, d), lambda i, k: (0, 0)),
                  pl.BlockSpec((d, th), lambda i, k: (0, k)),
                  pl.BlockSpec((d, th), lambda i, k: (0, k + nk)),
                  pl.BlockSpec((th, d), lambda i, k: (k, 0)),
                  pl.BlockSpec((1, d), lambda i, k: (0, 0))],
        out_specs=pl.BlockSpec((tm, d), lambda i, k: (i, 0)),
        out_shape=jax.ShapeDtypeStruct((n, d), F32),
        scratch_shapes=[pltpu.VMEM((tm, d), BF16), pltpu.VMEM((tm, d), F32)],
        compiler_params=_cparams("parallel", "arbitrary"),
        name="ffn",
    )(x, g1, wgu, wgu, wd, g2)


def _lane_row(width, pieces):
    row = jnp.zeros((1, width), F32)
    for off, vals in pieces:
        row = row.at[0, off:off + vals.shape[0]].set(vals.astype(F32))
    return row


def _layer_params(l, n_mix_pre, n_mix_post, n_ffn_pre, n_ffn_post, w_in, a_bi, a_bf, a_norm, b_w, b_scale,
                  c_qnorm, c_wuq, c_kvnorm, c_wuk, c_wuv, d_conv_w, d_conv_b, d_dt_bias, d_a_log, d_skip, d_norm,
                  w_branch, w_o, w_gu, w_down):
    offs = np.concatenate([[0], np.cumsum(IN_SPLITS)]).tolist()
    w_l = w_in[:, l, :]
    seg = [w_l[offs[i]:offs[i + 1]] for i in range(len(IN_SPLITS))]
    aq, ak, av, ao, ai, af, xb, cq, ckv, ckr, dz, dxbc, ddt, gates = seg
    z = lambda w: jnp.zeros((w, D_MODEL), F32)
    small = jnp.concatenate([ckr, ai, z(4), af, z(4), ddt, z(128 - SM_DT - 4)], 0)
    k1 = jnp.tile(ckr[:16], (8, 1))
    k2 = jnp.tile(ckr[16:], (8, 1))
    p = {}
    p["w_in"] = _cast_bf16(jnp.concatenate([aq, ak, av, ao, xb, cq, dxbc, dz, ckv, small, k1, k2, gates], 0))
    p["n_mix_pre"] = n_mix_pre[l][None]
    p["n_mix_post"] = n_mix_post[l][None]
    p["n_ffn_pre"] = n_ffn_pre[l][None]
    p["n_ffn_post"] = n_ffn_post[l][None]
    p["a_bias_row"] = _lane_row(128, [(SM_IG, a_bi[l]), (SM_FG, a_bf[l])])
    p["a_norm_row"] = a_norm[l][None]
    p["a_bi"], p["a_bf"], p["a_norm"] = a_bi[l], a_bf[l], a_norm[l]
    wblk = jnp.zeros((WIDTH, WIDTH), F32)
    for g in range(4):
        wblk = wblk.at[64 * g:64 * g + 64, 64 * g:64 * g + 64].set(b_w[l][g])
    p["b_wblk"] = wblk.astype(BF16)
    p["b_scale_row"] = b_scale[l][None]
    wuq = c_wuq[l].reshape(256, HEADS, 96)
    p["c_wn"] = wuq[:, :, :64].reshape(256, 256).astype(BF16)
    p["c_wn_t"] = p["c_wn"].T
    wx1 = wuq[:, :, 64:80]
    wx2 = wuq[:, :, 80:96]
    p["c_wx1_t"] = jnp.tile(wx1.reshape(256, 64), (1, 2)).astype(BF16).T
    p["c_wx2_t"] = jnp.tile(wx2.reshape(256, 64), (1, 2)).astype(BF16).T
    p["c_wxa"] = jnp.concatenate([wx1, wx1], -1).transpose(1, 0, 2).astype(BF16)
    p["c_wxb"] = jnp.concatenate([wx2, wx2], -1).transpose(1, 0, 2).astype(BF16)
    wuk_bd = jnp.zeros((256, 512), F32)
    for h in range(HEADS):
        wuk_bd = wuk_bd.at[64 * h:64 * h + 64, 128 * h:128 * h + 128].set(c_wuk[l][:, h, :].T)
    p["c_wuk_bd"] = wuk_bd.astype(BF16)
    p["c_wuk_bd_t"] = p["c_wuk_bd"].T
    p["c_wuv"] = c_wuv[l].reshape(C_KV_RANK, WIDTH).astype(BF16)
    p["c_wuv_t"] = p["c_wuv"].T
    p["c_qnorm"] = c_qnorm[l][None]
    p["c_kvnorm"] = c_kvnorm[l][None]
    p["d_conv_w"] = d_conv_w[l]
    p["d_conv_b"] = d_conv_b[l]
    p["d_conv_b_row"] = d_conv_b[l][None]
    p["d_bias_row"] = _lane_row(128, [(SM_DT, d_dt_bias[l])])
    p["d_alog_row"] = _lane_row(128, [(SM_DT, d_a_log[l])])
    p["d_alog_col"] = _lane_row(8, [(0, d_a_log[l])]).reshape(8, 1)
    p["d_skip_row"] = jnp.repeat(d_skip[l], DH)[None]
    p["d_norm_row"] = d_norm[l][None]
    p["d_dt_bias"], p["d_a_log"], p["d_skip"], p["d_norm"] = d_dt_bias[l], d_a_log[l], d_skip[l], d_norm[l]
    p["w_branch"] = w_branch[l].astype(BF16)
    p["w_o"] = w_o[l].astype(BF16)
    p["w_gu"] = w_gu[l].astype(BF16)
    p["w_down"] = w_down[l].astype(BF16)
    return p


def _rope_tables(pos):
    inv = ROPE_THETA ** (-jnp.arange(16, dtype=F32) / 16)
    ang = pos.astype(F32)[:, None] * inv[None, :]
    cos, sin = jnp.cos(ang), jnp.sin(ang)
    cos64, sin64 = jnp.tile(cos, (1, 4)), jnp.tile(sin, (1, 4))
    cs1 = jnp.concatenate([cos64, sin64], 1)
    cs2 = jnp.concatenate([-sin64, cos64], 1)
    csa = jnp.concatenate([cos, sin], 1)
    csb = jnp.concatenate([-sin, cos], 1)
    return cs1, cs2, cs1.T, cs2.T, csa, csb


def _head_lane_mask():
    lane = np.arange(128)
    return jnp.asarray(((lane % 64) // 16)[None, :] == np.arange(HEADS)[:, None], F32)[:, :, None]


def _prompt_layer(x, bsz, t_len, p, tabs, hm):
    u = _in_proj(x, p["n_mix_pre"], p["w_in"])
    out_a, c_pad, m_pad = _mlstm_prompt(u, bsz, t_len, p["a_bias_row"], p["a_norm_row"])
    out_b, pool1 = _pool_prompt(u, bsz, t_len, p["b_wblk"], p["b_scale_row"])
    out_d, conv1, ssm1 = _ssd_prompt(u, bsz, t_len, p["d_conv_w"], p["d_conv_b_row"], p["d_bias_row"],
                                     p["d_alog_row"], p["d_alog_col"], p["d_skip_row"], p["d_norm_row"])
    c_lat, k_rope, kcat, clat_t, q_t = _mla_prep(u, t_len, tabs, p["c_qnorm"], p["c_kvnorm"], p["c_wn_t"],
                                                 p["c_wx1_t"], p["c_wx2_t"], p["c_wuk_bd_t"], hm)
    out_c = _mla_prompt(q_t, kcat, clat_t, p["c_wuv_t"], bsz, t_len)
    x = _merge(x, (out_a, out_b, out_c, out_d), u, p["w_branch"], p["w_o"], p["n_mix_post"])
    x = _ffn(x, p["n_ffn_pre"], p["w_gu"], p["w_down"], p["n_ffn_post"])
    c1 = jnp.stack([c_pad[:, h, 64 * (h % 2):64 * (h % 2) + 64, :64] for h in range(HEADS)], 1)
    n1 = jnp.stack([c_pad[:, h, 64 * (h % 2):64 * (h % 2) + 64, 64] for h in range(HEADS)], 1)
    m1 = m_pad[:, :HEADS, 0]
    state = (c1, n1, m1, pool1, conv1, ssm1.reshape(bsz, HEADS, DH, D_STATE))
    rows = (c_lat.reshape(bsz, t_len, C_KV_RANK), k_rope.reshape(bsz, t_len, C_ROPE))
    return x, state, rows


def _sample_layer(x, p, layer, tabs_s, state, page_table, cache_ckv, cache_kr, past_len):
    c0, n0, m0, pool0, conv0, ssm0 = state
    bsz = x.shape[0]
    u = _in_proj(x, p["n_mix_pre"], p["w_in"])
    sm = u[:, U_SMALL:U_SMALL + 128]
    out_a, c1, n1, m1 = _mlstm_step(u[:, 0:256], u[:, 256:512], u[:, 512:768], u[:, 768:1024],
                                    sm[:, SM_IG:SM_IG + 4], sm[:, SM_FG:SM_FG + 4], c0, n0, m0,
                                    p["a_bi"], p["a_bf"], p["a_norm"])
    pooled, pool1 = _pool_step(u[:, U_XB:U_XB + WIDTH], pool0, past_len)
    out_b = _pool_lin(pooled.reshape(bsz, WIDTH), p["b_wblk"], p["b_scale_row"])
    out_d, conv1, ssm1 = _ssd_step(u[:, U_DXBC:U_DXBC + D_CONV_DIM], u[:, U_DZ:U_DZ + WIDTH],
                                   sm[:, SM_DT:SM_DT + 4], conv0, ssm0, p["d_conv_w"], p["d_conv_b"],
                                   p["d_dt_bias"], p["d_a_log"], p["d_skip"], p["d_norm"])
    c_lat, k_rope, qa, qr = _mla_prep_sample(u, tabs_s[4], tabs_s[5], p["c_qnorm"], p["c_kvnorm"], p["c_wn"],
                                             p["c_wxa"], p["c_wxb"], p["c_wuk_bd"])
    lane_pad = 256 - C_KV_RANK - C_ROPE
    q8 = jnp.pad(jnp.concatenate([qa, qr], -1).transpose(1, 0, 2), ((0, 0), (0, 8 - HEADS), (0, lane_pad)))
    k_new = jnp.pad(jnp.concatenate([c_lat, k_rope], -1), ((0, 0), (0, lane_pad))).reshape(bsz, 1, 256)
    out_c = _mla_decode(page_table, q8, k_new, p["c_wuv"], cache_ckv, cache_kr, layer)
    outs = (out_a.reshape(bsz, WIDTH), out_b, out_c.reshape(bsz, WIDTH), out_d.reshape(bsz, WIDTH))
    x = _merge(x, outs, u, p["w_branch"], p["w_o"], p["n_mix_post"])
    x = _ffn(x, p["n_ffn_pre"], p["w_gu"], p["w_down"], p["n_ffn_post"])
    new_state = (c1, n1.reshape(bsz, HEADS, DH), m1.reshape(bsz, HEADS), pool1, conv1, ssm1)
    rows = (c_lat.reshape(bsz, 1, C_KV_RANK), k_rope.reshape(bsz, 1, C_ROPE))
    return x, new_state, rows


def kernel(x_prompt, x_sample, state_mlstm_c, state_mlstm_n, state_mlstm_m, state_pool, cache_ckv, cache_krope,
           page_table, state_conv, state_ssm, n_mix_pre, n_mix_post, n_ffn_pre, n_ffn_post, w_in, a_bi, a_bf,
           a_norm, b_w, b_scale, c_qnorm, c_wuq, c_kvnorm, c_wuk, c_wuv, d_conv_w, d_conv_b, d_dt_bias, d_a_log,
           d_skip, d_norm, w_branch, w_o, w_gu, w_down):
    bp, t_len, d = x_prompt.shape
    bs = x_sample.shape[0]
    assert x_sample.shape[1] == 1 and d == D_MODEL
    depth = w_in.shape[0]
    w_in = jnp.transpose(w_in, (2, 0, 1))
    past_len = page_table.shape[1] * cache_ckv.shape[2]
    cache_krope = jnp.swapaxes(cache_krope, 2, 3)
    tabs_p = _rope_tables(jnp.arange(t_len))
    tabs_s = _rope_tables(jnp.full((1,), past_len))
    hm = _head_lane_mask()
    yp = x_prompt.reshape(bp * t_len, d)
    ys = x_sample.reshape(bs, d)
    st_p, st_s, rows_p, rows_s = [], [], [], []
    for l in range(depth):
        p = _layer_params(l, n_mix_pre, n_mix_post, n_ffn_pre, n_ffn_post, w_in, a_bi, a_bf, a_norm, b_w, b_scale,
                          c_qnorm, c_wuq, c_kvnorm, c_wuk, c_wuv, d_conv_w, d_conv_b, d_dt_bias, d_a_log, d_skip,
                          d_norm, w_branch, w_o, w_gu, w_down)
        yp, sp, rp = _prompt_layer(yp, bp, t_len, p, tabs_p, hm)
        in_state = (state_mlstm_c[l], state_mlstm_n[l], state_mlstm_m[l], state_pool[l], state_conv[l], state_ssm[l])
        ys, ss, rs = _sample_layer(ys, p, l, tabs_s, in_state, page_table, cache_ckv, cache_krope, past_len)
        st_p.append(sp)
        st_s.append(ss)
        rows_p.append(rp)
        rows_s.append(rs)

    stk = lambda lst, i: jnp.stack([s[i] for s in lst], 0)
    return (yp.reshape(bp, t_len, d), ys.reshape(bs, 1, d),
            stk(rows_p, 0), stk(rows_p, 1), stk(rows_s, 0), stk(rows_s, 1),
            stk(st_p, 0), stk(st_p, 1), stk(st_p, 2), stk(st_s, 0), stk(st_s, 1), stk(st_s, 2),
            stk(st_p, 3), stk(st_s, 3), stk(st_p, 4), stk(st_s, 4), stk(st_p, 5), stk(st_s, 5))
```
